```python
import math
import jax, jax.numpy as jnp
from jax import lax
import numpy as np

D_MODEL = 1024
BATCH = 16
SEQ = 4096
DEPTH = 1
DEC_BATCH = 128
DEC_SEQ = 4
PAST_LEN = 8192
PAGE_SIZE = 128

ATTN_WIDTH = D_MODEL // 2
N_HEADS = 8
HEAD_DIM = ATTN_WIDTH // N_HEADS
SSM_WIDTH = D_MODEL - ATTN_WIDTH
SSM_GROUP = 16
N_SSM_GROUPS = SSM_WIDTH // SSM_GROUP
SSM_STATE = 64
MIX_WIDTH = ATTN_WIDTH + SSM_WIDTH
IN_COLS = 3 * ATTN_WIDTH + N_HEADS + SSM_WIDTH
FORGET_BIAS_LO = 1.0
FORGET_BIAS_HI = 8.0
Q_BLOCK = 128
ATTN_SCALE = HEAD_DIM ** -0.5
N_EXPERTS = 32
TOP_K = 4
D_EXPERT = D_MODEL
SWIGLU_LIMIT = 7.0
SWIGLU_ALPHA = 1.702
LN_EPS = 1e-5
DN_ALPHA = (2.0 * DEPTH) ** 0.25
DN_BETA = (8.0 * DEPTH) ** -0.25

kernel_name = 'hymba_fox_s5_moe_step'


def layer_norm(x, g, b):
    xf = x.astype(jnp.float32)
    mu = jnp.mean(xf, axis=-1, keepdims=True)
    var = jnp.mean(jnp.square(xf - mu), axis=-1, keepdims=True)
    y = (xf - mu) * lax.rsqrt(var + LN_EPS) * g.astype(jnp.float32) + b.astype(jnp.float32)
    return y.astype(x.dtype)


def in_projection(x, w_in, b_forget):
    bsz, slen = x.shape[:2]
    h = jnp.einsum('bsd,de->bse', x, w_in)
    a = ATTN_WIDTH
    q = h[..., :a].reshape(bsz, slen, N_HEADS, HEAD_DIM)
    k = h[..., a:2 * a].reshape(bsz, slen, N_HEADS, HEAD_DIM)
    v = h[..., 2 * a:3 * a].reshape(bsz, slen, N_HEADS, HEAD_DIM)
    logf = jax.nn.log_sigmoid((h[..., 3 * a:3 * a + N_HEADS] + b_forget).astype(jnp.float32))
    u = h[..., 3 * a + N_HEADS:]
    return q, k, v, logf, u


def fox_prompt(q, k, v, logf):
    bsz, slen = q.shape[:2]
    c = jnp.cumsum(logf, axis=1)
    c_k = c.transpose(0, 2, 1)
    nb = slen // Q_BLOCK
    qb = q.reshape(bsz, nb, Q_BLOCK, N_HEADS, HEAD_DIM).transpose(1, 0, 2, 3, 4)
    cb = c.reshape(bsz, nb, Q_BLOCK, N_HEADS).transpose(1, 0, 2, 3)
    kpos = jnp.arange(slen)

    def block(args):
        qi, ci, bi = args
        qpos = bi * Q_BLOCK + jnp.arange(Q_BLOCK)
        s = jnp.einsum('bqhd,bkhd->bhqk', qi, k, preferred_element_type=jnp.float32) * ATTN_SCALE
        s = s + ci.transpose(0, 2, 1)[..., None] - c_k[:, :, None, :]
        s = jnp.where(kpos[None, :] <= qpos[:, None], s, -jnp.inf)
        p = jax.nn.softmax(s, axis=-1)
        return jnp.einsum('bhqk,bkhd->bqhd', p.astype(v.dtype), v)

    o = lax.map(block, (qb, cb, jnp.arange(nb)))
    return o.transpose(1, 0, 2, 3, 4).reshape(bsz, slen, ATTN_WIDTH)


def fox_sample(q, k, v, logf, k_past, v_past, logf_past):
    bsz, tlen = q.shape[:2]
    plen = k_past.shape[1]
    rev_excl = lax.cumsum(logf_past, axis=1, reverse=True) - logf_past
    c_new = jnp.cumsum(logf, axis=1).transpose(0, 2, 1)
    s_past = jnp.einsum('bqhd,bkhd->bhqk', q, k_past, preferred_element_type=jnp.float32) * ATTN_SCALE
    s_past = s_past + c_new[..., None] + rev_excl.transpose(0, 2, 1)[:, :, None, :]
    s_new = jnp.einsum('bqhd,bkhd->bhqk', q, k, preferred_element_type=jnp.float32) * ATTN_SCALE
    s_new = s_new + c_new[..., None] - c_new[:, :, None, :]
    tpos = jnp.arange(tlen)
    s_new = jnp.where(tpos[None, :] <= tpos[:, None], s_new, -jnp.inf)
    p = jax.nn.softmax(jnp.concatenate([s_past, s_new], axis=-1), axis=-1)
    o = (jnp.einsum('bhqk,bkhd->bqhd', p[..., :plen].astype(v.dtype), v_past)
         + jnp.einsum('bhqk,bkhd->bqhd', p[..., plen:].astype(v.dtype), v))
    return o.reshape(bsz, tlen, ATTN_WIDTH)


def s5_discretise(a_re, a_im, log_dt, b_re, b_im):
    a_re = a_re.astype(jnp.float32)
    a_im = a_im.astype(jnp.float32)
    b_re = b_re.astype(jnp.float32)
    b_im = b_im.astype(jnp.float32)
    dt = jnp.exp(log_dt.astype(jnp.float32))[:, None]
    mag = jnp.exp(a_re * dt)
    ar = mag * jnp.cos(a_im * dt)
    ai = mag * jnp.sin(a_im * dt)
    den = a_re * a_re + a_im * a_im
    cr = ((ar - 1.0) * a_re + ai * a_im) / den
    ci = (ai * a_re - (ar - 1.0) * a_im) / den
    br = cr[..., None] * b_re - ci[..., None] * b_im
    bi = cr[..., None] * b_im + ci[..., None] * b_re
    return ar, ai, br, bi


def _ssm_combine(left, right):
    a1r, a1i, b1r, b1i = left
    a2r, a2i, b2r, b2i = right
    return (a2r * a1r - a2i * a1i,
            a2r * a1i + a2i * a1r,
            a2r * b1r - a2i * b1i + b2r,
            a2r * b1i + a2i * b1r + b2i)


def s5_mixer(u, h0_re, h0_im, abar_re, abar_im, bbar_re, bbar_im, c_re, c_im, d, glu_w, glu_b):
    bsz, slen = u.shape[:2]
    uf = u.astype(jnp.float32).reshape(bsz, slen, N_SSM_GROUPS, SSM_GROUP)
    h0_re = h0_re.astype(jnp.float32)
    h0_im = h0_im.astype(jnp.float32)
    bu_re = jnp.einsum('bsgc,gnc->bsgn', uf, bbar_re)
    bu_im = jnp.einsum('bsgc,gnc->bsgn', uf, bbar_im)
    first_re = abar_re * h0_re - abar_im * h0_im + bu_re[:, 0]
    first_im = abar_re * h0_im + abar_im * h0_re + bu_im[:, 0]
    bu_re = bu_re.at[:, 0].set(first_re)
    bu_im = bu_im.at[:, 0].set(first_im)
    a_re_b = jnp.broadcast_to(abar_re, bu_re.shape)
    a_im_b = jnp.broadcast_to(abar_im, bu_im.shape)
    _, _, h_re, h_im = lax.associative_scan(_ssm_combine, (a_re_b, a_im_b, bu_re, bu_im), axis=1)
    y = (jnp.einsum('bsgn,gcn->bsgc', h_re, c_re.astype(jnp.float32))
         - jnp.einsum('bsgn,gcn->bsgc', h_im, c_im.astype(jnp.float32))
         + d.astype(jnp.float32) * uf)
    g = jax.nn.gelu(y, approximate=False)
    y = g * jax.nn.sigmoid(jnp.einsum('bsgc,gce->bsge', g, glu_w.astype(jnp.float32))
                           + glu_b.astype(jnp.float32))
    return y.reshape(bsz, slen, SSM_WIDTH).astype(u.dtype), h_re[:, -1], h_im[:, -1]


def moe_ffn(x, router_w, router_b, w_up, b_up, w_down, b_down):
    shp = x.shape
    xt = x.reshape(-1, shp[-1])
    logits = (xt @ router_w).astype(jnp.float32) + router_b.astype(jnp.float32)
    top_val, top_idx = lax.top_k(logits, TOP_K)
    top_w = jax.nn.softmax(top_val, axis=-1)
    gate = jnp.sum(jax.nn.one_hot(top_idx, N_EXPERTS, dtype=jnp.float32) * top_w[..., None], axis=1)
    y = jnp.zeros(xt.shape, jnp.float32)
    for e in range(N_EXPERTS):
        h = xt @ w_up[e] + b_up[e]
        x_glu = jnp.minimum(h[:, :D_EXPERT], SWIGLU_LIMIT)
        x_lin = jnp.clip(h[:, D_EXPERT:], -SWIGLU_LIMIT, SWIGLU_LIMIT)
        act = x_glu * jax.nn.sigmoid(SWIGLU_ALPHA * x_glu) * (x_lin + 1.0)
        y = y + gate[:, e:e + 1] * (act @ w_down[e] + b_down[e])
    return y.astype(x.dtype).reshape(shp)


def finish_layer(x, att, ssm, w_out, ln1_g, ln1_b, router_w, router_b, w_up, b_up, w_down, b_down, ln2_g, ln2_b):
    mixed = jnp.concatenate([att, ssm.astype(att.dtype)], axis=-1) @ w_out
    h = layer_norm(DN_ALPHA * x + mixed, ln1_g, ln1_b)
    return layer_norm(DN_ALPHA * h + moe_ffn(h, router_w, router_b, w_up, b_up, w_down, b_down), ln2_g, ln2_b)


def setup_inputs(seed: int = 0) -> dict:
    key = jax.random.key(seed)
    ks = jax.random.split(key, 32)
    nrm = jax.random.normal
    n_pages = PAST_LEN // PAGE_SIZE
    n_used = DEC_BATCH * n_pages
    n_pool = n_used + (n_used + 3) // 4
    f32 = jnp.float32
    x_prompt = nrm(ks[0], (BATCH, SEQ, D_MODEL), f32)
    x_sample = nrm(ks[1], (DEC_BATCH, DEC_SEQ, D_MODEL), f32)
    cache_k = nrm(ks[2], (n_pool, PAGE_SIZE, N_HEADS, HEAD_DIM), f32)
    cache_v = nrm(ks[3], (n_pool, PAGE_SIZE, N_HEADS, HEAD_DIM), f32)
    b_forget = jnp.linspace(FORGET_BIAS_LO, FORGET_BIAS_HI, N_HEADS, dtype=f32) + 0.1 * nrm(ks[9], (N_HEADS,), f32)
    cache_logf = jax.nn.log_sigmoid(b_forget[None, None, :] + 0.5 * nrm(ks[4], (n_pool, PAGE_SIZE, N_HEADS), f32))
    state_ssm_re = 0.3 * nrm(ks[5], (DEC_BATCH, N_SSM_GROUPS, SSM_STATE), f32)
    state_ssm_im = 0.3 * nrm(ks[6], (DEC_BATCH, N_SSM_GROUPS, SSM_STATE), f32)
    page_table = jax.random.permutation(ks[7], n_pool)[:n_used].reshape(DEC_BATCH, n_pages).astype(jnp.int32)
    w_in = nrm(ks[8], (D_MODEL, IN_COLS), f32) * D_MODEL ** -0.5
    n_idx = jnp.arange(SSM_STATE, dtype=f32)
    ssm_a_re = -0.5 + 0.01 * nrm(ks[10], (N_SSM_GROUPS, SSM_STATE), f32)
    ssm_a_im = math.pi * n_idx[None, :] + 0.01 * nrm(ks[11], (N_SSM_GROUPS, SSM_STATE), f32)
    ssm_log_dt = jax.random.uniform(ks[12], (N_SSM_GROUPS,), f32, math.log(1e-3), math.log(1e-1))
    ssm_b_re = nrm(ks[13], (N_SSM_GROUPS, SSM_STATE, SSM_GROUP), f32) * (2.0 * SSM_GROUP) ** -0.5
    ssm_b_im = nrm(ks[14], (N_SSM_GROUPS, SSM_STATE, SSM_GROUP), f32) * (2.0 * SSM_GROUP) ** -0.5
    ssm_c_re = nrm(ks[15], (N_SSM_GROUPS, SSM_GROUP, SSM_STATE), f32) * SSM_STATE ** -0.5
    ssm_c_im = nrm(ks[16], (N_SSM_GROUPS, SSM_GROUP, SSM_STATE), f32) * SSM_STATE ** -0.5
    ssm_d = nrm(ks[17], (N_SSM_GROUPS, SSM_GROUP), f32)
    ssm_glu_w = nrm(ks[18], (N_SSM_GROUPS, SSM_GROUP, SSM_GROUP), f32) * SSM_GROUP ** -0.5
    ssm_glu_b = 0.01 * nrm(ks[19], (N_SSM_GROUPS, SSM_GROUP), f32)
    w_out = nrm(ks[20], (MIX_WIDTH, D_MODEL), f32) * MIX_WIDTH ** -0.5 * DN_BETA
    ln1_g = 1.0 + 0.01 * nrm(ks[21], (D_MODEL,), f32)
    ln1_b = 0.01 * nrm(ks[22], (D_MODEL,), f32)
    router_w = nrm(ks[23], (D_MODEL, N_EXPERTS), f32) * D_MODEL ** -0.5
    router_b = 0.01 * nrm(ks[24], (N_EXPERTS,), f32)
    w_up = nrm(ks[25], (N_EXPERTS, D_MODEL, 2 * D_EXPERT), f32) * D_MODEL ** -0.5
    b_up = 0.01 * nrm(ks[26], (N_EXPERTS, 2 * D_EXPERT), f32)
    w_down = nrm(ks[27], (N_EXPERTS, D_EXPERT, D_MODEL), f32) * D_EXPERT ** -0.5 * DN_BETA
    b_down = 0.01 * nrm(ks[28], (N_EXPERTS, D_MODEL), f32)
    ln2_g = 1.0 + 0.01 * nrm(ks[29], (D_MODEL,), f32)
    ln2_b = 0.01 * nrm(ks[30], (D_MODEL,), f32)
    return {'x_prompt': x_prompt, 'x_sample': x_sample, 'cache_k': cache_k, 'cache_v': cache_v,
            'cache_logf': cache_logf, 'state_ssm_re': state_ssm_re, 'state_ssm_im': state_ssm_im,
            'page_table': page_table, 'w_in': w_in, 'b_forget': b_forget,
            'ssm_a_re': ssm_a_re, 'ssm_a_im': ssm_a_im, 'ssm_log_dt': ssm_log_dt,
            'ssm_b_re': ssm_b_re, 'ssm_b_im': ssm_b_im, 'ssm_c_re': ssm_c_re, 'ssm_c_im': ssm_c_im,
            'ssm_d': ssm_d, 'ssm_glu_w': ssm_glu_w, 'ssm_glu_b': ssm_glu_b, 'w_out': w_out,
            'ln1_g': ln1_g, 'ln1_b': ln1_b, 'router_w': router_w, 'router_b': router_b,
            'w_up': w_up, 'b_up': b_up, 'w_down': w_down, 'b_down': b_down,
            'ln2_g': ln2_g, 'ln2_b': ln2_b}


def reference(x_prompt, x_sample, cache_k, cache_v, cache_logf, state_ssm_re, state_ssm_im, page_table,
              w_in, b_forget, ssm_a_re, ssm_a_im, ssm_log_dt, ssm_b_re, ssm_b_im, ssm_c_re, ssm_c_im,
              ssm_d, ssm_glu_w, ssm_glu_b, w_out, ln1_g, ln1_b, router_w, router_b,
              w_up, b_up, w_down, b_down, ln2_g, ln2_b):
    abar_re, abar_im, bbar_re, bbar_im = s5_discretise(ssm_a_re, ssm_a_im, ssm_log_dt, ssm_b_re, ssm_b_im)

    q_p, k_prompt, v_prompt, logf_p, u_p = in_projection(x_prompt, w_in, b_forget)
    att_p = fox_prompt(q_p, k_prompt, v_prompt, logf_p)
    h0 = jnp.zeros((x_prompt.shape[0], N_SSM_GROUPS, SSM_STATE), jnp.float32)
    ssm_p, hre_p, him_p = s5_mixer(u_p, h0, h0, abar_re, abar_im, bbar_re, bbar_im,
                                   ssm_c_re, ssm_c_im, ssm_d, ssm_glu_w, ssm_glu_b)
    y_prompt = finish_layer(x_prompt, att_p, ssm_p, w_out, ln1_g, ln1_b, router_w, router_b,
                            w_up, b_up, w_down, b_down, ln2_g, ln2_b)

    dec_b, n_pages = page_table.shape
    plen = n_pages * PAGE_SIZE
    k_past = jnp.take(cache_k, page_table, axis=0).reshape(dec_b, plen, N_HEADS, HEAD_DIM)
    v_past = jnp.take(cache_v, page_table, axis=0).reshape(dec_b, plen, N_HEADS, HEAD_DIM)
    logf_past = jnp.take(cache_logf, page_table, axis=0).reshape(dec_b, plen, N_HEADS).astype(jnp.float32)
    q_s, k_sample, v_sample, logf_s, u_s = in_projection(x_sample, w_in, b_forget)
    att_s = fox_sample(q_s, k_sample, v_sample, logf_s, k_past, v_past, logf_past)
    ssm_s, hre_s, him_s = s5_mixer(u_s, state_ssm_re, state_ssm_im, abar_re, abar_im, bbar_re, bbar_im,
                                   ssm_c_re, ssm_c_im, ssm_d, ssm_glu_w, ssm_glu_b)
    y_sample = finish_layer(x_sample, att_s, ssm_s, w_out, ln1_g, ln1_b, router_w, router_b,
                            w_up, b_up, w_down, b_down, ln2_g, ln2_b)

    logf_prompt = logf_p.astype(cache_logf.dtype)
    logf_sample = logf_s.astype(cache_logf.dtype)
    ssm_re_prompt = hre_p.astype(state_ssm_re.dtype)
    ssm_im_prompt = him_p.astype(state_ssm_im.dtype)
    ssm_re_sample = hre_s.astype(state_ssm_re.dtype)
    ssm_im_sample = him_s.astype(state_ssm_im.dtype)
    return (y_prompt, y_sample, k_prompt, v_prompt, logf_prompt, ssm_re_prompt, ssm_im_prompt,
            k_sample, v_sample, logf_sample, ssm_re_sample, ssm_im_sample)
```

```python
import functools
import math

import jax
import jax.numpy as jnp
from jax import lax
from jax.experimental import pallas as pl
from jax.experimental.pallas import tpu as pltpu

N_HEADS = 8
HEAD_DIM = 64
ATTN_WIDTH = N_HEADS * HEAD_DIM
SSM_GROUP = 16
SSM_STATE = 64
TOP_K = 4
SWIGLU_LIMIT = 7.0
SWIGLU_ALPHA = 1.702
LN_EPS = 1e-5
DEPTH = 1
DN_ALPHA = (2.0 * DEPTH) ** 0.25
ATTN_SCALE = HEAD_DIM ** -0.5

LANES = 128
GROUPS_PER_LANE_BLOCK = LANES // SSM_GROUP
STATE_LANES = GROUPS_PER_LANE_BLOCK * 2 * SSM_STATE
VMEM_LIMIT_BYTES = 56 * 1024 * 1024

BF16 = jnp.bfloat16
F32 = jnp.float32
NEG_INF = float("-inf")


def _cparams(*sem):
    return pltpu.CompilerParams(dimension_semantics=sem, vmem_limit_bytes=VMEM_LIMIT_BYTES)


def _split3(x):
    hi = x.astype(BF16)
    r1 = x - hi.astype(F32)
    mid = r1.astype(BF16)
    lo = (r1 - mid.astype(F32)).astype(BF16)
    return hi, mid, lo


def _dot(a, b):
    return jnp.dot(a, b, preferred_element_type=F32)


def _dot_nt(a, b):
    return lax.dot_general(a, b, (((1,), (1,)), ((), ())), preferred_element_type=F32)


def _inproj_kernel(x_ref, wqkv_ref, wf_ref, wu_ref, bf_ref, tri_ref,
                   q_ref, k_ref, v_ref, kb_ref, vb_ref, logf_ref, c_ref, u_ref, carry_ref,
                   *, chunk_len):
    i = pl.program_id(1)
    xb = x_ref[...].astype(BF16)
    qkv = _dot(xb, wqkv_ref[...])
    a = ATTN_WIDTH
    q_ref[...] = qkv[:, :a].astype(BF16)
    k = qkv[:, a:2 * a]
    v = qkv[:, 2 * a:]
    k_ref[...] = k
    v_ref[...] = v
    kb_ref[...] = k.astype(BF16)
    vb_ref[...] = v.astype(BF16)
    f = _dot(xb, wf_ref[...]) + bf_ref[...]
    logf = jnp.minimum(f, 0.0) - jnp.log1p(jnp.exp(-jnp.abs(f)))
    logf_ref[...] = logf[:, :N_HEADS]

    @pl.when(i == 0)
    def _():
        carry_ref[...] = jnp.zeros_like(carry_ref)
    hi, mid, lo = _split3(logf)
    tri = tri_ref[...]
    c = _dot(tri, hi) + _dot(tri, mid) + _dot(tri, lo) + carry_ref[...]
    c_ref[...] = c[:, :N_HEADS]
    carry_ref[...] = c[-1:, :]

    u = _dot(xb, wu_ref[...])
    if chunk_len is None:
        u_ref[...] = u
    else:
        u_ref[...] = u.reshape(u.shape[0] // chunk_len, chunk_len, u.shape[1])


def in_projection(x, w_in, b_forget, *, tm, chunk_len):
    nseq, slen, d = x.shape
    m = nseq * slen
    nt = slen // tm
    a = ATTN_WIDTH
    x2 = x.reshape(m, d)
    col_scale = jnp.concatenate([jnp.full((a,), ATTN_SCALE, F32), jnp.ones((2 * a,), F32)])
    wqkv = (w_in[:, :3 * a] * col_scale).astype(BF16)
    wf = jnp.pad(w_in[:, 3 * a:3 * a + N_HEADS], ((0, 0), (0, LANES - N_HEADS))).astype(BF16)
    wu = w_in[:, 3 * a + N_HEADS:].astype(BF16)
    ssm_w = wu.shape[1]
    bf = jnp.pad(b_forget.astype(F32), (0, LANES - N_HEADS)).reshape(1, LANES)
    tri = jnp.tril(jnp.ones((tm, tm), F32)).astype(BF16)

    row = lambda b, i: (b * nt + i, 0)
    full = lambda b, i: (0, 0)
    tok = lambda w, dt: jax.ShapeDtypeStruct((m, w), dt)
    if chunk_len is None:
        u_shape = tok(ssm_w, F32)
        u_spec = pl.BlockSpec((tm, ssm_w), row)
    else:
        u_shape = jax.ShapeDtypeStruct((slen // chunk_len, nseq, chunk_len, ssm_w), F32)
        u_spec = pl.BlockSpec((tm // chunk_len, None, chunk_len, ssm_w), lambda b, i: (i, b, 0, 0))
    outs = pl.pallas_call(
        functools.partial(_inproj_kernel, chunk_len=chunk_len),
        grid=(nseq, nt),
        in_specs=[pl.BlockSpec((tm, d), row),
                  pl.BlockSpec((d, 3 * a), full),
                  pl.BlockSpec((d, LANES), full),
                  pl.BlockSpec((d, ssm_w), full),
                  pl.BlockSpec((1, LANES), full),
                  pl.BlockSpec((tm, tm), full)],
        out_specs=[pl.BlockSpec((tm, a), row)] * 5
                  + [pl.BlockSpec((tm, N_HEADS), row)] * 2 + [u_spec],
        out_shape=[tok(a, BF16), tok(a, F32), tok(a, F32), tok(a, BF16), tok(a, BF16),
                   tok(N_HEADS, F32), tok(N_HEADS, F32), u_shape],
        scratch_shapes=[pltpu.VMEM((1, LANES), F32)],
        compiler_params=_cparams("parallel", "arbitrary"),
        name="in_projection",
    )(x2, wqkv, wf, wu, bf, tri)
    return outs


def _fox_prompt_kernel(q_ref, k_ref, v_ref, c_ref, ct_ref, o_ref, m_ref, l_ref, acc_ref):
    i = pl.program_id(1)
    kk = pl.program_id(2)
    nk = pl.num_programs(2)
    tq = q_ref.shape[0]
    tk = k_ref.shape[0]

    @pl.when(kk == 0)
    def _():
        m_ref[...] = jnp.full_like(m_ref, NEG_INF)
        l_ref[...] = jnp.zeros_like(l_ref)
        acc_ref[...] = jnp.zeros_like(acc_ref)

    def step(masked):
        lane = lax.broadcasted_iota(jnp.int32, (1, LANES), 1)
        low = lane < HEAD_DIM
        if masked:
            qpos = lax.broadcasted_iota(jnp.int32, (tq, tk), 0)
            kpos = lax.broadcasted_iota(jnp.int32, (tq, tk), 1)
            causal = kpos <= qpos
        for blk in range(N_HEADS // 2):
            sl = slice(blk * LANES, (blk + 1) * LANES)
            q = q_ref[:, sl]
            k = k_ref[:, sl]
            v = v_ref[:, sl]
            pvs, alphas = [], []
            for half in range(2):
                h = 2 * blk + half
                sel = low if half == 0 else jnp.logical_not(low)
                qh = jnp.where(sel, q, jnp.zeros_like(q))
                s = _dot_nt(qh, k)
                s = s + c_ref[:, h:h + 1] - ct_ref[h:h + 1, :]
                if masked:
                    s = jnp.where(causal, s, NEG_INF)
                m_old = m_ref[:, h:h + 1]
                m_new = jnp.maximum(m_old, jnp.max(s, axis=1, keepdims=True))
                alpha = jnp.exp(m_old - m_new)
                p = jnp.exp(s - m_new)
                l_ref[:, h:h + 1] = alpha * l_ref[:, h:h + 1] + jnp.sum(p, axis=1, keepdims=True)
                m_ref[:, h:h + 1] = m_new
                pvs.append(_dot(p.astype(BF16), v))
                alphas.append(alpha)
            alpha2 = jnp.where(low, alphas[0], alphas[1])
            pv2 = jnp.where(low, pvs[0], pvs[1])
            acc_ref[:, sl] = alpha2 * acc_ref[:, sl] + pv2

    @pl.when(kk < i)
    def _():
        step(False)

    @pl.when(kk == i)
    def _():
        step(True)

    @pl.when(kk == nk - 1)
    def _():
        lane = lax.broadcasted_iota(jnp.int32, (1, LANES), 1)
        low = lane < HEAD_DIM
        for blk in range(N_HEADS // 2):
            sl = slice(blk * LANES, (blk + 1) * LANES)
            l2 = jnp.where(low, l_ref[:, 2 * blk:2 * blk + 1], l_ref[:, 2 * blk + 1:2 * blk + 2])
            o_ref[:, sl] = (acc_ref[:, sl] / l2).astype(o_ref.dtype)


def fox_prompt_attention(qb, kb, vb, c, nseq, slen, *, tq):
    m = nseq * slen
    nq = slen // tq
    a = ATTN_WIDTH
    ct = c.reshape(nseq, slen, N_HEADS).transpose(0, 2, 1)
    qmap = lambda b, i, kk: (b * nq + i, 0)
    kmap = lambda b, i, kk: (b * nq + jnp.minimum(kk, i), 0)
    return pl.pallas_call(
        _fox_prompt_kernel,
        grid=(nseq, nq, nq),
        in_specs=[pl.BlockSpec((tq, a), qmap),
                  pl.BlockSpec((tq, a), kmap),
                  pl.BlockSpec((tq, a), kmap),
                  pl.BlockSpec((tq, N_HEADS), qmap),
                  pl.BlockSpec((None, N_HEADS, tq), lambda b, i, kk: (b, 0, jnp.minimum(kk, i)))],
        out_specs=pl.BlockSpec((tq, a), qmap),
        out_shape=jax.ShapeDtypeStruct((m, a), BF16),
        scratch_shapes=[pltpu.VMEM((tq, N_HEADS), F32), pltpu.VMEM((tq, N_HEADS), F32),
                        pltpu.VMEM((tq, a), F32)],
        compiler_params=_cparams("parallel", "parallel", "arbitrary"),
        name="fox_prompt",
    )(qb, kb, vb, c, ct)


def _s5_tables(a_re, a_im, log_dt, b_re, b_im, c_re, c_im, chunk_len):
    g, n = a_re.shape
    cdim = b_re.shape[2]
    nblk = g // GROUPS_PER_LANE_BLOCK
    gb = GROUPS_PER_LANE_BLOCK
    a_re = a_re.astype(F32)
    a_im = a_im.astype(F32)
    b_re = b_re.astype(F32)
    b_im = b_im.astype(F32)
    c_re = c_re.astype(F32)
    c_im = c_im.astype(F32)
    dt = jnp.exp(log_dt.astype(F32))[:, None]
    mag = jnp.exp(a_re * dt)
    ar = mag * jnp.cos(a_im * dt)
    ai = mag * jnp.sin(a_im * dt)
    den = a_re * a_re + a_im * a_im
    cr = ((ar - 1.0) * a_re + ai * a_im) / den
    ci = (ai * a_re - (ar - 1.0) * a_im) / den
    bbr = cr[..., None] * b_re - ci[..., None] * b_im
    bbi = cr[..., None] * b_im + ci[..., None] * b_re
    pr = [jnp.ones_like(ar)]
    pi = [jnp.zeros_like(ai)]
    for _ in range(chunk_len):
        pr_n = pr[-1] * ar - pi[-1] * ai
        pi_n = pr[-1] * ai + pi[-1] * ar
        pr.append(pr_n)
        pi.append(pi_n)
    pr = jnp.stack(pr)
    pi = jnp.stack(pi)
    L = chunk_len
    pbr = pr[:L, :, :, None] * bbr[None] - pi[:L, :, :, None] * bbi[None]
    pbi = pr[:L, :, :, None] * bbi[None] + pi[:L, :, :, None] * bbr[None]
    kmat = jnp.einsum('gon,lgni->lgoi', c_re, pbr) - jnp.einsum('gon,lgni->lgoi', c_im, pbi)
    tdiff = jnp.arange(L)[None, :] - jnp.arange(L)[:, None]
    kt = jnp.where((tdiff >= 0)[:, :, None, None, None], kmat[jnp.clip(tdiff, 0, L - 1)], 0.0)
    eye = jnp.eye(gb, dtype=F32)
    kt = kt.reshape(L, L, nblk, gb, cdim, cdim)
    tmat = jnp.einsum('stqgoi,gh->qsgitho', kt, eye).reshape(nblk, L * gb * cdim, L * gb * cdim)
    wr = pbr[::-1].reshape(L, nblk, gb, n, cdim)
    wi = pbi[::-1].reshape(L, nblk, gb, n, cdim)
    wri = jnp.stack([wr, wi], axis=3)
    wmat = jnp.einsum('sqgrni,gh->qsgihrn', wri, eye).reshape(nblk, L * gb * cdim, gb * 2 * n)
    cpr = c_re[None] * pr[1:, :, None, :] - c_im[None] * pi[1:, :, None, :]
    cpi = c_re[None] * pi[1:, :, None, :] + c_im[None] * pr[1:, :, None, :]
    vri = jnp.stack([cpr, -cpi], axis=1).reshape(L, 2, nblk, gb, cdim, n)
    vmat = jnp.einsum('trqgon,gh->qgrntho', vri, eye).reshape(nblk, gb * 2 * n, L * gb * cdim)
    al_r = pr[L].reshape(nblk, gb, n)
    al_i = pi[L].reshape(nblk, gb, n)
    a1 = jnp.stack([al_r, al_r], axis=2).reshape(nblk, 1, gb * 2 * n)
    a2 = jnp.stack([-al_i, al_i], axis=2).reshape(nblk, 1, gb * 2 * n)
    return tmat, wmat, vmat, a1, a2


def _gelu_exact(y):
    return 0.5 * y * (1.0 + lax.erf(y * (2.0 ** -0.5)))


def _s5_kernel(u_ref, h0_ref, t_ref, w_ref, v_ref, a1_ref, a2_ref, d_ref, gw_ref, gb_ref,
               y_ref, hout_ref, h_scr, hp_scr, y_scr, *, chunk_len, nseq, precise_state):
    i = pl.program_id(1)
    L = chunk_len
    rows = u_ref.shape[0]
    n = rows // L
    nj = n // nseq

    @pl.when(i == 0)
    def _():
        h_scr[...] = h0_ref[...]

    xs = [u_ref[pl.ds(t, n, stride=L), :] for t in range(L)] if L > 1 else [u_ref[...]]
    x32 = jnp.concatenate(xs, axis=1)
    xb = x32.astype(BF16)
    w = w_ref[0]
    s = _dot(xb, w)
    if precise_state:
        xlo = (x32 - xb.astype(F32)).astype(BF16)
        s = s + _dot(xlo, w) + _dot(xb, w_ref[1])
    a1 = a1_ref[...]
    a2 = a2_ref[...]
    h = h_scr[...]
    width = h.shape[1]
    is_re = (lax.broadcasted_iota(jnp.int32, (1, width), 1) % (2 * SSM_STATE)) < SSM_STATE
    for j in range(nj):
        hp_scr[j * nseq:(j + 1) * nseq, :] = h
        swapped = jnp.where(is_re, pltpu.roll(h, width - SSM_STATE, 1), pltpu.roll(h, SSM_STATE, 1))
        h = a1 * h + a2 * swapped + s[j * nseq:(j + 1) * nseq, :]
    h_scr[...] = h
    y = _dot(xb, t_ref[...]) + _dot(hp_scr[...].astype(BF16), v_ref[...])
    if L > 1:
        for t in range(L):
            y_scr[pl.ds(t, n, stride=L), :] = y[:, t * LANES:(t + 1) * LANES]
        ytok = y_scr[...]
    else:
        ytok = y
    ytok = ytok + d_ref[...] * u_ref[...]
    g = _gelu_exact(ytok)
    z = _dot(g.astype(BF16), gw_ref[...]) + gb_ref[...]
    y_ref[...] = (g * jax.nn.sigmoid(z)).astype(y_ref.dtype)

    @pl.when(i == pl.num_programs(1) - 1)
    def _():
        hout_ref[...] = h


def s5_mixer(u_rows, h0_re, h0_im, tables, ssm_d, glu_w, glu_b, *, nseq, chunk_len, chunks_per_tile,
             precise_state):
    tmat, wmat, vmat, a1, a2 = tables
    mrows, width = u_rows.shape
    nblk = width // LANES
    gb = GROUPS_PER_LANE_BLOCK
    L = chunk_len
    rows_tile = chunks_per_tile * nseq * L
    nt = mrows // rows_tile
    g_total = h0_re.shape[1]
    n = h0_re.shape[2]
    h0 = jnp.stack([h0_re.astype(F32), h0_im.astype(F32)], axis=2)
    h0 = h0.reshape(nseq, nblk, gb * 2 * n).transpose(1, 0, 2)
    drow = ssm_d.astype(F32).reshape(nblk, 1, LANES)
    gwb = jnp.einsum('qgce,gh->qgche', glu_w.astype(F32).reshape(nblk, gb, SSM_GROUP, SSM_GROUP),
                     jnp.eye(gb, dtype=F32)).reshape(nblk, LANES, LANES).astype(BF16)
    gbrow = glu_b.astype(F32).reshape(nblk, 1, LANES)
    kl = L * LANES
    sl = STATE_LANES
    blk = lambda q, i: (q, 0, 0)
    w_hi = wmat.astype(BF16)
    if precise_state:
        w_parts = jnp.stack([w_hi, (wmat - w_hi.astype(F32)).astype(BF16)], axis=1)
    else:
        w_parts = w_hi[:, None]
    nparts = w_parts.shape[1]
    y, hout = pl.pallas_call(
        functools.partial(_s5_kernel, chunk_len=L, nseq=nseq, precise_state=precise_state),
        grid=(nblk, nt),
        in_specs=[pl.BlockSpec((rows_tile, LANES), lambda q, i: (i, q)),
                  pl.BlockSpec((None, nseq, sl), blk),
                  pl.BlockSpec((None, kl, kl), blk),
                  pl.BlockSpec((None, nparts, kl, sl), lambda q, i: (q, 0, 0, 0)),
                  pl.BlockSpec((None, sl, kl), blk),
                  pl.BlockSpec((None, 1, sl), blk),
                  pl.BlockSpec((None, 1, sl), blk),
                  pl.BlockSpec((None, 1, LANES), blk),
                  pl.BlockSpec((None, LANES, LANES), blk),
                  pl.BlockSpec((None, 1, LANES), blk)],
        out_specs=[pl.BlockSpec((rows_tile, LANES), lambda q, i: (i, q)),
                   pl.BlockSpec((None, nseq, sl), blk)],
        out_shape=[jax.ShapeDtypeStruct((mrows, width), BF16),
                   jax.ShapeDtypeStruct((nblk, nseq, sl), F32)],
        scratch_shapes=[pltpu.VMEM((nseq, sl), F32),
                        pltpu.VMEM((chunks_per_tile * nseq, sl), F32),
                        pltpu.VMEM((rows_tile, LANES), F32)],
        compiler_params=_cparams("parallel", "arbitrary"),
        name="s5_mixer",
    )(u_rows, h0, tmat.astype(BF16), w_parts, vmat.astype(BF16), a1, a2, drow, gwb, gbrow)
    hout = hout.transpose(1, 0, 2).reshape(nseq, g_total, 2, n)
    return y, hout[:, :, 0, :], hout[:, :, 1, :]


QROWS_PAD = 16


def _fox_sample_kernel(pt_ref, q_ref, kn_ref, vn_ref, lfn_ref, usup_ref, *rest, pages_per_step, n_new):
    pp = pages_per_step
    k_refs = rest[:pp]
    v_refs = rest[pp:2 * pp]
    lf_refs = rest[2 * pp:3 * pp]
    o_ref, m_ref, l_ref, acc_ref, carry_ref, qbd_ref, ccol_ref = rest[3 * pp:]
    g = pl.program_id(1)
    nrow = n_new * N_HEADS
    width = q_ref.shape[1]

    sub_h = lax.broadcasted_iota(jnp.int32, (N_HEADS, width), 0)
    lane_h = lax.broadcasted_iota(jnp.int32, (N_HEADS, width), 1) // HEAD_DIM
    headmask = sub_h == lane_h
    e_sel = (lax.broadcasted_iota(jnp.int32, (nrow, N_HEADS), 0) % N_HEADS
             == lax.broadcasted_iota(jnp.int32, (nrow, N_HEADS), 1))
    e_f32 = e_sel.astype(F32)
    e_bf = e_sel.astype(BF16)

    def new_cumsum_rows():
        lfn = lfn_ref[...]
        rows, run = [], None
        for t in range(n_new):
            run = lfn[t:t + 1, :] if run is None else run + lfn[t:t + 1, :]
            rows.append(run)
        return rows

    @pl.when(g == 0)
    def _():
        m_ref[...] = jnp.full_like(m_ref, NEG_INF)
        l_ref[...] = jnp.zeros_like(l_ref)
        acc_ref[...] = jnp.zeros_like(acc_ref)
        carry_ref[...] = jnp.zeros_like(carry_ref)
        q = q_ref[...].astype(F32)
        cn = new_cumsum_rows()
        for qi in range(n_new):
            qrow = jnp.broadcast_to(q[qi:qi + 1, :], (N_HEADS, width))
            qbd_ref[qi * N_HEADS:(qi + 1) * N_HEADS, :] = jnp.where(headmask, qrow, 0.0).astype(BF16)
            ccol_ref[qi * N_HEADS:(qi + 1) * N_HEADS, :] = jnp.sum(
                e_f32[:N_HEADS, :] * cn[qi], axis=1, keepdims=True)

    def online_update(s, v_bf):
        m_old = m_ref[...]
        m_new = jnp.maximum(m_old, jnp.max(s, axis=1, keepdims=True))
        alpha = jnp.exp(m_old - m_new)
        p = jnp.exp(s - m_new)
        l_ref[...] = alpha * l_ref[...] + jnp.sum(p, axis=1, keepdims=True)
        m_ref[...] = m_new
        acc_ref[...] = alpha * acc_ref[...] + _dot(p.astype(BF16), v_bf)

    qbd = qbd_ref[...]
    ccol = ccol_ref[...]
    usup = usup_ref[...]
    for pi in range(pp):
        kb = k_refs[pi][...].astype(BF16)
        s = _dot_nt(qbd, kb)
        lf = lf_refs[pi][...]
        hi, mid, lo = _split3(lf)
        rs = _dot(usup, hi) + _dot(usup, mid) + _dot(usup, lo) + carry_ref[...]
        r_hi, r_mid, r_lo = _split3(rs)
        bias = _dot_nt(e_bf, r_hi) + _dot_nt(e_bf, r_mid) + _dot_nt(e_bf, r_lo)
        carry_ref[...] = carry_ref[...] + jnp.sum(lf, axis=0, keepdims=True)
        online_update(s + bias + ccol, v_refs[pi][...].astype(BF16))

    @pl.when(g == pl.num_programs(1) - 1)
    def _():
        cn = new_cumsum_rows()
        s = _dot_nt(qbd, kn_ref[...])
        kj = lax.broadcasted_iota(jnp.int32, (nrow, QROWS_PAD), 1)
        qi = lax.broadcasted_iota(jnp.int32, (nrow, QROWS_PAD), 0) // N_HEADS
        bias = jnp.zeros((nrow, QROWS_PAD), F32)
        for t in range(n_new):
            col = jnp.sum(e_f32 * cn[t], axis=1, keepdims=True)
            bias = jnp.where(kj == t, col, bias)
        s = s + ccol - bias
        s = jnp.where(kj <= qi, s, NEG_INF)
        online_update(s, vn_ref[...])
        o = acc_ref[...] / l_ref[...]
        for q_i in range(n_new):
            blk = o[q_i * N_HEADS:(q_i + 1) * N_HEADS, :]
            o_ref[q_i:q_i + 1, :] = jnp.sum(jnp.where(headmask, blk, 0.0), axis=0,
                                            keepdims=True).astype(o_ref.dtype)


def fox_sample_attention(qb, kb_new, vb_new, logf_new, cache_k, cache_v, cache_logf, page_table,
                         *, pages_per_step):
    nb, n_new, a = qb.shape
    n_pool, page, _, _ = cache_k.shape
    n_pages = page_table.shape[1]
    pp = pages_per_step
    ck = cache_k.reshape(n_pool, page, a)
    cv = cache_v.reshape(n_pool, page, a)
    pad = ((0, 0), (0, QROWS_PAD - n_new), (0, 0))
    kn = jnp.pad(kb_new, pad)
    vn = jnp.pad(vb_new, pad)
    usup = jnp.triu(jnp.ones((page, page), F32), 1).astype(BF16)
    pt = page_table.reshape(-1).astype(jnp.int32)

    def page_map(pi):
        return lambda b, g, pt_ref: (pt_ref[b * n_pages + (n_pages - 1 - (g * pp + pi))], 0, 0)

    per_b = lambda b, g, pt_ref: (b, 0, 0)
    grid_spec = pltpu.PrefetchScalarGridSpec(
        num_scalar_prefetch=1,
        grid=(nb, n_pages // pp),
        in_specs=[pl.BlockSpec((None, n_new, a), per_b),
                  pl.BlockSpec((None, QROWS_PAD, a), per_b),
                  pl.BlockSpec((None, QROWS_PAD, a), per_b),
                  pl.BlockSpec((None, n_new, N_HEADS), per_b),
                  pl.BlockSpec((page, page), lambda b, g, pt_ref: (0, 0))]
                 + [pl.BlockSpec((None, page, a), page_map(pi)) for pi in range(pp)]
                 + [pl.BlockSpec((None, page, a), page_map(pi)) for pi in range(pp)]
                 + [pl.BlockSpec((None, page, N_HEADS), page_map(pi)) for pi in range(pp)],
        out_specs=pl.BlockSpec((None, n_new, a), per_b),
        scratch_shapes=[pltpu.VMEM((n_new * N_HEADS, 1), F32), pltpu.VMEM((n_new * N_HEADS, 1), F32),
                        pltpu.VMEM((n_new * N_HEADS, a), F32), pltpu.VMEM((1, N_HEADS), F32),
                        pltpu.VMEM((n_new * N_HEADS, a), BF16), pltpu.VMEM((n_new * N_HEADS, 1), F32)],
    )
    return pl.pallas_call(
        functools.partial(_fox_sample_kernel, pages_per_step=pp, n_new=n_new),
        grid_spec=grid_spec,
        out_shape=jax.ShapeDtypeStruct((nb, n_new, a), BF16),
        compiler_params=_cparams("parallel", "arbitrary"),
        name="fox_sample",
    )(pt, qb, kn, vn, logf_new, usup, *([ck] * pp), *([cv] * pp), *([cache_logf] * pp))


def _layer_norm(x, g, b):
    mu = jnp.mean(x, axis=-1, keepdims=True)
    xc = x - mu
    var = jnp.mean(xc * xc, axis=-1, keepdims=True)
    return xc * lax.rsqrt(var + LN_EPS) * g + b


def _outproj_router_kernel(*refs, n_experts, aliased, n_tiles):
    if aliased:
        refs = refs[1:]
    h_ref = refs[9]

    @pl.when(pl.program_id(0) < n_tiles)
    def _():
        _outproj_router_tile(*refs, n_experts=n_experts)

    @pl.when(pl.program_id(0) >= n_tiles)
    def _():
        h_ref[...] = jnp.zeros_like(h_ref)


def _outproj_router_tile(att_ref, ssm_ref, x_ref, woa_ref, wos_ref, g_ref, b_ref, rw_ref, rb_ref,
                         h_ref, idx_ref, w_ref, *, n_experts):
    tm = x_ref.shape[0]
    att = att_ref[...]
    ssm = ssm_ref[...].reshape(tm, ssm_ref.shape[-1])
    mixed = _dot(att, woa_ref[...]) + _dot(ssm, wos_ref[...])
    h = _layer_norm(DN_ALPHA * x_ref[...] + mixed, g_ref[...], b_ref[...])
    h_ref[...] = h
    h_hi = h.astype(BF16)
    h_lo = (h - h_hi.astype(F32)).astype(BF16)
    logits = _dot(h_hi, rw_ref[0]) + _dot(h_lo, rw_ref[0]) + _dot(h_hi, rw_ref[1]) + rb_ref[...]
    lane = lax.broadcasted_iota(jnp.int32, logits.shape, 1)
    logits = jnp.where(lane < n_experts, logits, NEG_INF)
    vals, idxs = [], []
    for _ in range(TOP_K):
        mx = jnp.max(logits, axis=1, keepdims=True)
        ix = jnp.min(jnp.where(logits == mx, lane, LANES), axis=1, keepdims=True)
        vals.append(mx)
        idxs.append(ix)
        logits = jnp.where(lane == ix, NEG_INF, logits)
    es = [jnp.exp(v - vals[0]) for v in vals]
    denom = es[0] + es[1] + es[2] + es[3]
    lane8 = lax.broadcasted_iota(jnp.int32, (tm, idx_ref.shape[1]), 1)
    idx_out = jnp.zeros(lane8.shape, jnp.int32)
    w_out = jnp.zeros(lane8.shape, F32)
    for k in range(TOP_K):
        idx_out = jnp.where(lane8 == k, idxs[k], idx_out)
        w_out = jnp.where(lane8 == k, es[k] / denom, w_out)
    idx_ref[...] = idx_out
    w_ref[...] = w_out


ROUTE_COLS = 8


def outproj_router(att, ssm, x2, w_out, ln_g, ln_b, router_w, router_b, *, tm, m_total, row_block_offset,
                   h_all, ssm_chunked):
    m, d = x2.shape
    a = att.shape[1]
    n_experts = router_w.shape[1]
    nt = m // tm
    woa = w_out[:a].astype(BF16)
    wos = w_out[a:].astype(BF16)
    rw = jnp.pad(router_w.astype(F32), ((0, 0), (0, LANES - n_experts)))
    rw_hi = rw.astype(BF16)
    rw_parts = jnp.stack([rw_hi, (rw - rw_hi.astype(F32)).astype(BF16)])
    rb = jnp.pad(router_b.astype(F32), (0, LANES - n_experts)).reshape(1, LANES)
    aliased = h_all is not None
    n_fill = 0 if aliased else m_total // tm - nt - row_block_offset
    clamp = lambda i: jnp.minimum(i, nt - 1)
    row = lambda i: (clamp(i), 0)
    full = lambda i: (0, 0)
    if ssm_chunked is None:
        ssm_spec = pl.BlockSpec((tm, ssm.shape[-1]), row)
    else:
        nseq, slen, L = ssm_chunked
        tps = slen // tm
        ssm_spec = pl.BlockSpec((tm // L, None, L, ssm.shape[-1]),
                                lambda i: (clamp(i) % tps, clamp(i) // tps, 0, 0))
    in_specs = [pl.BlockSpec((tm, a), row), ssm_spec, pl.BlockSpec((tm, d), row),
                pl.BlockSpec((a, d), full), pl.BlockSpec((w_out.shape[0] - a, d), full),
                pl.BlockSpec((1, d), full), pl.BlockSpec((1, d), full),
                pl.BlockSpec((2, d, LANES), lambda i: (0, 0, 0)), pl.BlockSpec((1, LANES), full)]
    args = [att, ssm, x2, woa, wos, ln_g.astype(F32).reshape(1, d), ln_b.astype(F32).reshape(1, d), rw_parts, rb]
    aliases = {}
    if aliased:
        in_specs = [pl.BlockSpec(memory_space=pl.ANY)] + in_specs
        args = [h_all] + args
        aliases = {0: 0}
    return pl.pallas_call(
        functools.partial(_outproj_router_kernel, n_experts=n_experts, aliased=aliased, n_tiles=nt),
        grid=(nt + n_fill,),
        in_specs=in_specs,
        out_specs=[pl.BlockSpec((tm, d), lambda i: (i + row_block_offset, 0)),
                   pl.BlockSpec((tm, ROUTE_COLS), row), pl.BlockSpec((tm, ROUTE_COLS), row)],
        out_shape=[jax.ShapeDtypeStruct((m_total, d), F32),
                   jax.ShapeDtypeStruct((m, ROUTE_COLS), jnp.int32),
                   jax.ShapeDtypeStruct((m, ROUTE_COLS), F32)],
        input_output_aliases=aliases,
        compiler_params=_cparams("parallel"),
        name="outproj_router",
    )(*args)


def _gather_rows_kernel(idx_ref, src_ref, out_ref, sem):
    rows = out_ref.shape[0]

    def row_copy(r, src_row):
        return pltpu.make_async_copy(src_ref.at[pl.ds(src_row, 1), :], out_ref.at[pl.ds(r, 1), :], sem)

    def start(r, carry):
        row_copy(r, idx_ref[0, r]).start()
        return carry

    def wait(r, carry):
        row_copy(r, 0).wait()
        return carry

    lax.fori_loop(0, rows, start, 0, unroll=8)
    lax.fori_loop(0, rows, wait, 0, unroll=8)


def gather_rows(src, idx, *, rows_per_tile):
    n = idx.shape[0]
    d = src.shape[1]
    nt = n // rows_per_tile
    return pl.pallas_call(
        _gather_rows_kernel,
        grid=(nt,),
        in_specs=[pl.BlockSpec((None, 1, rows_per_tile), lambda i: (i, 0, 0), memory_space=pltpu.SMEM),
                  pl.BlockSpec(memory_space=pl.ANY)],
        out_specs=pl.BlockSpec((rows_per_tile, d), lambda i: (i, 0)),
        out_shape=jax.ShapeDtypeStruct((n, d), src.dtype),
        scratch_shapes=[pltpu.SemaphoreType.DMA],
        compiler_params=_cparams("arbitrary"),
        name="moe_gather",
    )(idx.reshape(nt, 1, rows_per_tile), src)


def _expert_kernel(te_ref, tv_ref, x_ref, gate_ref, wup_ref, bup_ref, wdn_ref, bdn_ref, y_ref):
    i = pl.program_id(0)
    de = wdn_ref.shape[0]

    @pl.when(tv_ref[i] > 0)
    def _():
        xb = x_ref[...].astype(BF16)
        hcat = _dot(xb, wup_ref[...]) + bup_ref[...]
        x_glu = jnp.minimum(hcat[:, :de], SWIGLU_LIMIT)
        x_lin = jnp.clip(hcat[:, de:], -SWIGLU_LIMIT, SWIGLU_LIMIT)
        act = x_glu * jax.nn.sigmoid(SWIGLU_ALPHA * x_glu) * (x_lin + 1.0)
        y = _dot(act.astype(BF16), wdn_ref[...]) + bdn_ref[...]
        y_ref[...] = y * gate_ref[...]

    @pl.when(tv_ref[i] == 0)
    def _():
        y_ref[...] = jnp.zeros_like(y_ref)


def expert_mlp(xs, gate_sorted, tile_expert, tile_valid, w_up, b_up, w_down, b_down, *, tm):
    npad, d = xs.shape
    n_e, _, d2 = w_up.shape
    de = w_down.shape[1]
    dout = w_down.shape[2]
    nt = npad // tm
    grid_spec = pltpu.PrefetchScalarGridSpec(
        num_scalar_prefetch=2,
        grid=(nt,),
        in_specs=[pl.BlockSpec((tm, d), lambda i, te, tv: (i, 0)),
                  pl.BlockSpec((tm, 1), lambda i, te, tv: (i, 0)),
                  pl.BlockSpec((None, d, d2), lambda i, te, tv: (te[i], 0, 0)),
                  pl.BlockSpec((None, 1, d2), lambda i, te, tv: (te[i], 0, 0)),
                  pl.BlockSpec((None, de, dout), lambda i, te, tv: (te[i], 0, 0)),
                  pl.BlockSpec((None, 1, dout), lambda i, te, tv: (te[i], 0, 0))],
        out_specs=pl.BlockSpec((tm, dout), lambda i, te, tv: (i, 0)),
    )
    return pl.pallas_call(
        _expert_kernel,
        grid_spec=grid_spec,
        out_shape=jax.ShapeDtypeStruct((npad, dout), F32),
        compiler_params=_cparams("arbitrary"),
        name="moe_experts",
    )(tile_expert, tile_valid, xs, gate_sorted.reshape(npad, 1), w_up.astype(BF16),
      b_up.astype(F32).reshape(n_e, 1, d2), w_down.astype(BF16), b_down.astype(F32).reshape(n_e, 1, dout))


def _combine_ln_kernel(pos_ref, h_ref, g_ref, b_ref, ys_ref, o_ref, buf, sem):
    tm = h_ref.shape[0]
    nrows = TOP_K * tm

    def row_copy(r, src_row):
        return pltpu.make_async_copy(ys_ref.at[pl.ds(src_row, 1), :], buf.at[pl.ds(r, 1), :], sem)

    def start(r, carry):
        row_copy(r, pos_ref[0, r]).start()
        return carry

    def wait(r, carry):
        row_copy(r, 0).wait()
        return carry

    lax.fori_loop(0, nrows, start, 0, unroll=8)
    lax.fori_loop(0, nrows, wait, 0, unroll=8)
    moe = buf[0:tm, :]
    for k in range(1, TOP_K):
        moe = moe + buf[k * tm:(k + 1) * tm, :]
    o_ref[...] = _layer_norm(DN_ALPHA * h_ref[...] + moe, g_ref[...], b_ref[...])


def combine_ln(h_all, ys, pos, ln_g, ln_b, *, tm, m, row_block_offset):
    d = h_all.shape[1]
    nt = m // tm
    pos_t = pos.reshape(nt, tm, TOP_K).transpose(0, 2, 1).reshape(nt, 1, TOP_K * tm)
    return pl.pallas_call(
        _combine_ln_kernel,
        grid=(nt,),
        in_specs=[pl.BlockSpec((None, 1, TOP_K * tm), lambda i: (i, 0, 0), memory_space=pltpu.SMEM),
                  pl.BlockSpec((tm, d), lambda i: (i + row_block_offset, 0)),
                  pl.BlockSpec((1, d), lambda i: (0, 0)),
                  pl.BlockSpec((1, d), lambda i: (0, 0)),
                  pl.BlockSpec(memory_space=pl.ANY)],
        out_specs=pl.BlockSpec((tm, d), lambda i: (i, 0)),
        out_shape=jax.ShapeDtypeStruct((m, d), F32),
        scratch_shapes=[pltpu.VMEM((TOP_K * tm, d), F32), pltpu.SemaphoreType.DMA],
        compiler_params=_cparams("arbitrary"),
        name="moe_combine_ln",
    )(pos_t, h_all, ln_g.astype(F32).reshape(1, d), ln_b.astype(F32).reshape(1, d), ys)


def _dispatch_plan(top_idx, top_w, n_experts, tm):
    m = top_idx.shape[0]
    n = m * TOP_K
    flat_e = top_idx.reshape(n)
    order = jnp.argsort(flat_e, stable=True).astype(jnp.int32)
    sel = (top_idx[:, :, None] == jnp.arange(n_experts, dtype=jnp.int32)[None, None, :]).sum(axis=1).astype(jnp.int32)
    counts = sel.sum(axis=0)
    ntile_e = (counts + tm - 1) // tm
    tile_end = jnp.cumsum(ntile_e)
    tile_start = tile_end - ntile_e
    ustart = jnp.cumsum(counts) - counts
    n_tiles = n // tm + n_experts
    tile_ids = jnp.arange(n_tiles, dtype=jnp.int32)
    tile_expert = jnp.minimum(jnp.searchsorted(tile_end, tile_ids, side='right'), n_experts - 1).astype(jnp.int32)
    tile_valid = (tile_ids < tile_end[-1]).astype(jnp.int32)
    slot = jnp.arange(n_tiles * tm, dtype=jnp.int32)
    st = slot // tm
    se = tile_expert[st]
    local = slot - tile_start[se] * tm
    valid = (local < counts[se]) & (tile_valid[st] > 0)
    src_assign = order[jnp.clip(ustart[se] + local, 0, n - 1)]
    tok_sorted = jnp.where(valid, src_assign // TOP_K, 0).astype(jnp.int32)
    gate_sorted = jnp.where(valid, top_w.reshape(n)[src_assign], 0.0).astype(F32)
    rank = jnp.cumsum(sel, axis=0) - sel
    pos = tile_start[top_idx] * tm + jnp.take_along_axis(rank, top_idx, axis=1)
    return tok_sorted, gate_sorted, tile_expert, tile_valid, pos.astype(jnp.int32)


S5_CHUNK = 16
TOKEN_TILE = 512
ATTN_TILE = 512
S5_CHUNKS_PER_TILE = 16
PAGES_PER_STEP = 8
EXPERT_TILE = 512
GATHER_TILE = 512
COMBINE_TILE = 256


def kernel(x_prompt, x_sample, cache_k, cache_v, cache_logf, state_ssm_re, state_ssm_im, page_table, w_in, b_forget, ssm_a_re, ssm_a_im, ssm_log_dt, ssm_b_re, ssm_b_im, ssm_c_re, ssm_c_im, ssm_d, ssm_glu_w, ssm_glu_b, w_out, ln1_g, ln1_b, router_w, router_b, w_up, b_up, w_down, b_down, ln2_g, ln2_b):
    bsz, slen, d = x_prompt.shape
    dec_b, dec_t, _ = x_sample.shape
    m_p = bsz * slen
    m_s = dec_b * dec_t
    m_all = m_p + m_s
    n_groups, n_state = ssm_a_re.shape
    n_experts = router_w.shape[1]
    tm = TOKEN_TILE
    assert m_s == tm and slen % tm == 0 and m_p % tm == 0
    assert (m_all * TOP_K) % EXPERT_TILE == 0 and m_p % COMBINE_TILE == 0 and m_s % COMBINE_TILE == 0

    qb, k_p, v_p, kb, vb, logf_p, c_p, u_p = in_projection(x_prompt, w_in, b_forget, tm=tm, chunk_len=S5_CHUNK)
    att_p = fox_prompt_attention(qb, kb, vb, c_p, bsz, slen, tq=ATTN_TILE)
    tables_p = _s5_tables(ssm_a_re, ssm_a_im, ssm_log_dt, ssm_b_re, ssm_b_im, ssm_c_re, ssm_c_im, S5_CHUNK)
    zeros_state = jnp.zeros((bsz, n_groups, n_state), F32)
    ssm_p, hre_p, him_p = s5_mixer(u_p.reshape(m_p, u_p.shape[-1]), zeros_state, zeros_state, tables_p,
                                   ssm_d, ssm_glu_w, ssm_glu_b, nseq=bsz, chunk_len=S5_CHUNK,
                                   chunks_per_tile=S5_CHUNKS_PER_TILE, precise_state=False)

    qs, k_s, v_s, ksb, vsb, logf_s, _, u_s = in_projection(x_sample.reshape(1, m_s, d), w_in, b_forget,
                                                            tm=tm, chunk_len=None)
    a = ATTN_WIDTH
    att_s = fox_sample_attention(qs.reshape(dec_b, dec_t, a), ksb.reshape(dec_b, dec_t, a),
                                 vsb.reshape(dec_b, dec_t, a), logf_s.reshape(dec_b, dec_t, N_HEADS),
                                 cache_k, cache_v, cache_logf, page_table, pages_per_step=PAGES_PER_STEP)
    tables_s = _s5_tables(ssm_a_re, ssm_a_im, ssm_log_dt, ssm_b_re, ssm_b_im, ssm_c_re, ssm_c_im, dec_t)
    ssm_s, hre_s, him_s = s5_mixer(u_s, state_ssm_re, state_ssm_im, tables_s, ssm_d, ssm_glu_w, ssm_glu_b,
                                   nseq=dec_b, chunk_len=dec_t, chunks_per_tile=1, precise_state=True)

    ssm_p4 = ssm_p.reshape(slen // S5_CHUNK, bsz, S5_CHUNK, ssm_p.shape[-1])
    h_all, idx_p, w_p = outproj_router(att_p, ssm_p4, x_prompt.reshape(m_p, d), w_out, ln1_g, ln1_b,
                                       router_w, router_b, tm=tm, m_total=m_all, row_block_offset=0,
                                       h_all=None, ssm_chunked=(bsz, slen, S5_CHUNK))
    h_all, idx_s, w_s = outproj_router(att_s.reshape(m_s, a), ssm_s, x_sample.reshape(m_s, d), w_out, ln1_g, ln1_b,
                                       router_w, router_b, tm=tm, m_total=m_all, row_block_offset=m_p // tm,
                                       h_all=h_all, ssm_chunked=None)

    top_idx = jnp.concatenate([idx_p, idx_s], axis=0)[:, :TOP_K]
    top_w = jnp.concatenate([w_p, w_s], axis=0)[:, :TOP_K]
    tok_sorted, gate_sorted, tile_expert, tile_valid, pos = _dispatch_plan(top_idx, top_w, n_experts, EXPERT_TILE)
    xs = gather_rows(h_all, tok_sorted, rows_per_tile=GATHER_TILE)
    ys = expert_mlp(xs, gate_sorted, tile_expert, tile_valid, w_up, b_up, w_down, b_down, tm=EXPERT_TILE)
    y_p = combine_ln(h_all, ys, pos[:m_p], ln2_g, ln2_b, tm=COMBINE_TILE, m=m_p, row_block_offset=0)
    y_s = combine_ln(h_all, ys, pos[m_p:], ln2_g, ln2_b, tm=COMBINE_TILE, m=m_s,
                     row_block_offset=m_p // COMBINE_TILE)

    hd = (N_HEADS, HEAD_DIM)
    return (y_p.reshape(bsz, slen, d), y_s.reshape(dec_b, dec_t, d),
            k_p.reshape(bsz, slen, *hd), v_p.reshape(bsz, slen, *hd),
            logf_p.reshape(bsz, slen, N_HEADS).astype(cache_logf.dtype),
            hre_p.astype(state_ssm_re.dtype), him_p.astype(state_ssm_im.dtype),
            k_s.reshape(dec_b, dec_t, *hd), v_s.reshape(dec_b, dec_t, *hd),
            logf_s.reshape(dec_b, dec_t, N_HEADS).astype(cache_logf.dtype),
            hre_s.astype(state_ssm_re.dtype), him_s.astype(state_ssm_im.dtype))
```

```python
import functools
import math

import jax
import jax.numpy as jnp
from jax import lax
from jax.experimental import pallas as pl
from jax.experimental.pallas import tpu as pltpu

N_HEADS = 8
HEAD_DIM = 64
ATTN_WIDTH = N_HEADS * HEAD_DIM
SSM_GROUP = 16
SSM_STATE = 64
TOP_K = 4
SWIGLU_LIMIT = 7.0
SWIGLU_ALPHA = 1.702
LN_EPS = 1e-5
DEPTH = 1
DN_ALPHA = (2.0 * DEPTH) ** 0.25
ATTN_SCALE = HEAD_DIM ** -0.5

LANES = 128
GROUPS_PER_LANE_BLOCK = LANES // SSM_GROUP
STATE_LANES = GROUPS_PER_LANE_BLOCK * 2 * SSM_STATE
VMEM_LIMIT_BYTES = 56 * 1024 * 1024

BF16 = jnp.bfloat16
F32 = jnp.float32
NEG_INF = float("-inf")


def _cparams(*sem):
    return pltpu.CompilerParams(dimension_semantics=sem, vmem_limit_bytes=VMEM_LIMIT_BYTES)


def _split3(x):
    hi = x.astype(BF16)
    r1 = x - hi.astype(F32)
    mid = r1.astype(BF16)
    lo = (r1 - mid.astype(F32)).astype(BF16)
    return hi, mid, lo


def _dot(a, b):
    return jnp.dot(a, b, preferred_element_type=F32)


def _dot_nt(a, b):
    return lax.dot_general(a, b, (((1,), (1,)), ((), ())), preferred_element_type=F32)


LOG2E = math.log2(math.e)
AUG_C_LANE = HEAD_DIM
AUG_ONE_LANE = HEAD_DIM + 3


def _aug_tables():
    e = jnp.zeros((LANES, 2 * N_HEADS * LANES), F32)
    ones = jnp.zeros((1, 2 * N_HEADS * LANES), F32)
    koff = N_HEADS * LANES
    for h in range(N_HEADS):
        for p in range(3):
            src = p * N_HEADS + h
            e = e.at[src, h * LANES + AUG_C_LANE + p].set(1.0)
            e = e.at[src, koff + h * LANES + AUG_ONE_LANE + p].set(-1.0)
            ones = ones.at[0, h * LANES + AUG_ONE_LANE + p].set(1.0)
            ones = ones.at[0, koff + h * LANES + AUG_C_LANE + p].set(1.0)
    return e.astype(BF16), ones


def _spread_heads(x):
    low = lax.broadcasted_iota(jnp.int32, (1, LANES), 1) < HEAD_DIM
    blocks = []
    for j in range(x.shape[1] // LANES):
        pair = x[:, j * LANES:(j + 1) * LANES]
        blocks.append(jnp.where(low, pair, 0.0))
        blocks.append(jnp.where(low, pltpu.roll(pair, HEAD_DIM, 1), 0.0))
    return jnp.concatenate(blocks, axis=1)


def _inproj_kernel(x_ref, wqkv_ref, wf_ref, wu_ref, bf_ref, *rest, chunk_len, aug):
    if aug:
        tri_ref, e_ref, ones_ref, q_ref, k_ref, v_ref, kb_ref, vb_ref, logf_ref, u_ref, carry_ref = rest
    else:
        q_ref, k_ref, v_ref, kb_ref, vb_ref, logf_ref, u_ref = rest
    xb = x_ref[...].astype(BF16)
    qkv = _dot(xb, wqkv_ref[...])
    a = ATTN_WIDTH
    q = qkv[:, :a]
    k = qkv[:, a:2 * a]
    v = qkv[:, 2 * a:]
    k_ref[...] = k
    v_ref[...] = v
    vb_ref[...] = v.astype(BF16)
    f = _dot(xb, wf_ref[...]) + bf_ref[...]
    logf = jnp.minimum(f, 0.0) - jnp.log1p(jnp.exp(-jnp.abs(f)))
    logf_ref[...] = logf[:, :N_HEADS]

    if aug:
        @pl.when(pl.program_id(1) == 0)
        def _():
            carry_ref[...] = jnp.zeros_like(carry_ref)
        hi, mid, lo = _split3(logf)
        tri = tri_ref[...]
        c = _dot(tri, hi) + _dot(tri, mid) + _dot(tri, lo) + carry_ref[...]
        carry_ref[...] = c[-1:, :]
        c_hi, c_mid, c_lo = _split3(c * LOG2E)
        lane = lax.broadcasted_iota(jnp.int32, (1, LANES), 1)
        pieces = jnp.where(lane < N_HEADS, c_hi.astype(F32),
                           jnp.where(lane < 2 * N_HEADS, pltpu.roll(c_mid.astype(F32), N_HEADS, 1),
                                     jnp.where(lane < 3 * N_HEADS, pltpu.roll(c_lo.astype(F32), 2 * N_HEADS, 1),
                                               0.0)))
        extra = _dot(pieces.astype(BF16), e_ref[...]) + ones_ref[...]
        half = N_HEADS * LANES
        q_ref[...] = (_spread_heads(q) + extra[:, :half]).astype(BF16)
        kb_ref[...] = (_spread_heads(k) + extra[:, half:]).astype(BF16)
    else:
        q_ref[...] = q.astype(BF16)
        kb_ref[...] = k.astype(BF16)

    u = _dot(xb, wu_ref[...])
    if chunk_len is None:
        u_ref[...] = u
    else:
        u_ref[...] = u.reshape(u.shape[0] // chunk_len, chunk_len, u.shape[1])


def in_projection(x, w_in, b_forget, *, tm, chunk_len, aug):
    nseq, slen, d = x.shape
    m = nseq * slen
    nt = slen // tm
    a = ATTN_WIDTH
    x2 = x.reshape(m, d)
    q_scale = ATTN_SCALE * LOG2E if aug else ATTN_SCALE
    col_scale = jnp.concatenate([jnp.full((a,), q_scale, F32), jnp.ones((2 * a,), F32)])
    wqkv = (w_in[:, :3 * a] * col_scale).astype(BF16)
    wf = jnp.pad(w_in[:, 3 * a:3 * a + N_HEADS], ((0, 0), (0, LANES - N_HEADS))).astype(BF16)
    wu = w_in[:, 3 * a + N_HEADS:].astype(BF16)
    ssm_w = wu.shape[1]
    bf = jnp.pad(b_forget.astype(F32), (0, LANES - N_HEADS)).reshape(1, LANES)

    row = lambda b, i: (b * nt + i, 0)
    full = lambda b, i: (0, 0)
    tok = lambda w, dt: jax.ShapeDtypeStruct((m, w), dt)
    if chunk_len is None:
        u_shape = tok(ssm_w, F32)
        u_spec = pl.BlockSpec((tm, ssm_w), row)
    else:
        u_shape = jax.ShapeDtypeStruct((slen // chunk_len, nseq, chunk_len, ssm_w), F32)
        u_spec = pl.BlockSpec((tm // chunk_len, None, chunk_len, ssm_w), lambda b, i: (i, b, 0, 0))
    in_specs = [pl.BlockSpec((tm, d), row),
                pl.BlockSpec((d, 3 * a), full),
                pl.BlockSpec((d, LANES), full),
                pl.BlockSpec((d, ssm_w), full),
                pl.BlockSpec((1, LANES), full)]
    args = [x2, wqkv, wf, wu, bf]
    qk_w = a
    scratch = []
    if aug:
        e_mat, ones_row = _aug_tables()
        qk_w = N_HEADS * LANES
        in_specs += [pl.BlockSpec((tm, tm), full), pl.BlockSpec((LANES, 2 * qk_w), full),
                     pl.BlockSpec((1, 2 * qk_w), full)]
        args += [jnp.tril(jnp.ones((tm, tm), F32)).astype(BF16), e_mat, ones_row]
        scratch = [pltpu.VMEM((1, LANES), F32)]
    return pl.pallas_call(
        functools.partial(_inproj_kernel, chunk_len=chunk_len, aug=aug),
        grid=(nseq, nt),
        in_specs=in_specs,
        out_specs=[pl.BlockSpec((tm, qk_w), row), pl.BlockSpec((tm, a), row), pl.BlockSpec((tm, a), row),
                   pl.BlockSpec((tm, qk_w), row), pl.BlockSpec((tm, a), row),
                   pl.BlockSpec((tm, N_HEADS), row), u_spec],
        out_shape=[tok(qk_w, BF16), tok(a, F32), tok(a, F32), tok(qk_w, BF16), tok(a, BF16),
                   tok(N_HEADS, F32), u_shape],
        scratch_shapes=scratch,
        compiler_params=_cparams("parallel", "arbitrary"),
        name="in_projection",
    )(*args)


def _fox_prompt_kernel(q_ref, k_ref, v_ref, o_ref, m_ref, l_ref, acc_ref):
    i = pl.program_id(1)
    kk = pl.program_id(2)
    nk = pl.num_programs(2)
    tq = q_ref.shape[0]
    tk = k_ref.shape[0]

    @pl.when(kk == 0)
    def _():
        m_ref[...] = jnp.full_like(m_ref, NEG_INF)
        l_ref[...] = jnp.zeros_like(l_ref)
        acc_ref[...] = jnp.zeros_like(acc_ref)

    def step(masked):
        lane = lax.broadcasted_iota(jnp.int32, (1, LANES), 1)
        low = lane < HEAD_DIM
        if masked:
            qpos = lax.broadcasted_iota(jnp.int32, (tq, tk), 0)
            kpos = lax.broadcasted_iota(jnp.int32, (tq, tk), 1)
            causal = kpos <= qpos
        for blk in range(N_HEADS // 2):
            sl = slice(blk * LANES, (blk + 1) * LANES)
            v = v_ref[:, sl]
            pvs, alphas = [], []
            for half in range(2):
                h = 2 * blk + half
                hs = slice(h * LANES, (h + 1) * LANES)
                s = _dot_nt(q_ref[:, hs], k_ref[:, hs])
                if masked:
                    s = jnp.where(causal, s, NEG_INF)
                m_old = m_ref[:, h:h + 1]
                m_new = jnp.maximum(m_old, jnp.max(s, axis=1, keepdims=True))
                alpha = jnp.exp2(m_old - m_new)
                p = jnp.exp2(s - m_new)
                l_ref[:, h:h + 1] = alpha * l_ref[:, h:h + 1] + jnp.sum(p, axis=1, keepdims=True)
                m_ref[:, h:h + 1] = m_new
                pvs.append(_dot(p.astype(BF16), v))
                alphas.append(alpha)
            alpha2 = jnp.where(low, alphas[0], alphas[1])
            pv2 = jnp.where(low, pvs[0], pvs[1])
            acc_ref[:, sl] = alpha2 * acc_ref[:, sl] + pv2

    @pl.when(kk < i)
    def _():
        step(False)

    @pl.when(kk == i)
    def _():
        step(True)

    @pl.when(kk == nk - 1)
    def _():
        lane = lax.broadcasted_iota(jnp.int32, (1, LANES), 1)
        low = lane < HEAD_DIM
        for blk in range(N_HEADS // 2):
            sl = slice(blk * LANES, (blk + 1) * LANES)
            l2 = jnp.where(low, l_ref[:, 2 * blk:2 * blk + 1], l_ref[:, 2 * blk + 1:2 * blk + 2])
            o_ref[:, sl] = (acc_ref[:, sl] / l2).astype(o_ref.dtype)


def fox_prompt_attention(q_aug, k_aug, vb, nseq, slen, *, tq):
    m = nseq * slen
    nq = slen // tq
    a = ATTN_WIDTH
    aw = q_aug.shape[1]
    qmap = lambda b, i, kk: (b * nq + i, 0)
    kmap = lambda b, i, kk: (b * nq + jnp.minimum(kk, i), 0)
    return pl.pallas_call(
        _fox_prompt_kernel,
        grid=(nseq, nq, nq),
        in_specs=[pl.BlockSpec((tq, aw), qmap),
                  pl.BlockSpec((tq, aw), kmap),
                  pl.BlockSpec((tq, a), kmap)],
        out_specs=pl.BlockSpec((tq, a), qmap),
        out_shape=jax.ShapeDtypeStruct((m, a), BF16),
        scratch_shapes=[pltpu.VMEM((tq, N_HEADS), F32), pltpu.VMEM((tq, N_HEADS), F32),
                        pltpu.VMEM((tq, a), F32)],
        compiler_params=_cparams("parallel", "parallel", "arbitrary"),
        name="fox_prompt",
    )(q_aug, k_aug, vb)


def _s5_tables(a_re, a_im, log_dt, b_re, b_im, c_re, c_im, chunk_len):
    g, n = a_re.shape
    cdim = b_re.shape[2]
    nblk = g // GROUPS_PER_LANE_BLOCK
    gb = GROUPS_PER_LANE_BLOCK
    a_re = a_re.astype(F32)
    a_im = a_im.astype(F32)
    b_re = b_re.astype(F32)
    b_im = b_im.astype(F32)
    c_re = c_re.astype(F32)
    c_im = c_im.astype(F32)
    dt = jnp.exp(log_dt.astype(F32))[:, None]
    mag = jnp.exp(a_re * dt)
    ar = mag * jnp.cos(a_im * dt)
    ai = mag * jnp.sin(a_im * dt)
    den = a_re * a_re + a_im * a_im
    cr = ((ar - 1.0) * a_re + ai * a_im) / den
    ci = (ai * a_re - (ar - 1.0) * a_im) / den
    bbr = cr[..., None] * b_re - ci[..., None] * b_im
    bbi = cr[..., None] * b_im + ci[..., None] * b_re
    pr = [jnp.ones_like(ar)]
    pi = [jnp.zeros_like(ai)]
    for _ in range(chunk_len):
        pr_n = pr[-1] * ar - pi[-1] * ai
        pi_n = pr[-1] * ai + pi[-1] * ar
        pr.append(pr_n)
        pi.append(pi_n)
    pr = jnp.stack(pr)
    pi = jnp.stack(pi)
    L = chunk_len
    pbr = pr[:L, :, :, None] * bbr[None] - pi[:L, :, :, None] * bbi[None]
    pbi = pr[:L, :, :, None] * bbi[None] + pi[:L, :, :, None] * bbr[None]
    kmat = jnp.einsum('gon,lgni->lgoi', c_re, pbr) - jnp.einsum('gon,lgni->lgoi', c_im, pbi)
    tdiff = jnp.arange(L)[None, :] - jnp.arange(L)[:, None]
    kt = jnp.where((tdiff >= 0)[:, :, None, None, None], kmat[jnp.clip(tdiff, 0, L - 1)], 0.0)
    eye = jnp.eye(gb, dtype=F32)
    kt = kt.reshape(L, L, nblk, gb, cdim, cdim)
    tmat = jnp.einsum('stqgoi,gh->qsgitho', kt, eye).reshape(nblk, L * gb * cdim, L * gb * cdim)
    wr = pbr[::-1].reshape(L, nblk, gb, n, cdim)
    wi = pbi[::-1].reshape(L, nblk, gb, n, cdim)
    wri = jnp.stack([wr, wi], axis=3)
    wmat = jnp.einsum('sqgrni,gh->qsgihrn', wri, eye).reshape(nblk, L * gb * cdim, gb * 2 * n)
    cpr = c_re[None] * pr[1:, :, None, :] - c_im[None] * pi[1:, :, None, :]
    cpi = c_re[None] * pi[1:, :, None, :] + c_im[None] * pr[1:, :, None, :]
    vri = jnp.stack([cpr, -cpi], axis=1).reshape(L, 2, nblk, gb, cdim, n)
    vmat = jnp.einsum('trqgon,gh->qgrntho', vri, eye).reshape(nblk, gb * 2 * n, L * gb * cdim)
    al_r = pr[L].reshape(nblk, gb, n)
    al_i = pi[L].reshape(nblk, gb, n)
    a1 = jnp.stack([al_r, al_r], axis=2).reshape(nblk, 1, gb * 2 * n)
    a2 = jnp.stack([-al_i, al_i], axis=2).reshape(nblk, 1, gb * 2 * n)
    return tmat, wmat, vmat, a1, a2


def _gelu_exact(y):
    return 0.5 * y * (1.0 + lax.erf(y * (2.0 ** -0.5)))


def _s5_kernel(u_ref, h0_ref, t_ref, w_ref, v_ref, a1_ref, a2_ref, d_ref, gw_ref, gb_ref,
               y_ref, hout_ref, h_scr, hp_scr, y_scr, *, chunk_len, nseq, precise_state):
    i = pl.program_id(1)
    L = chunk_len
    rows = u_ref.shape[0]
    n = rows // L
    nj = n // nseq

    @pl.when(i == 0)
    def _():
        h_scr[...] = h0_ref[...]

    xs = [u_ref[pl.ds(t, n, stride=L), :] for t in range(L)] if L > 1 else [u_ref[...]]
    x32 = jnp.concatenate(xs, axis=1)
    xb = x32.astype(BF16)
    w = w_ref[0]
    s = _dot(xb, w)
    if precise_state:
        xlo = (x32 - xb.astype(F32)).astype(BF16)
        s = s + _dot(xlo, w) + _dot(xb, w_ref[1])
    a1 = a1_ref[...]
    a2 = a2_ref[...]
    h = h_scr[...]
    width = h.shape[1]
    is_re = (lax.broadcasted_iota(jnp.int32, (1, width), 1) % (2 * SSM_STATE)) < SSM_STATE
    for j in range(nj):
        hp_scr[j * nseq:(j + 1) * nseq, :] = h
        swapped = jnp.where(is_re, pltpu.roll(h, width - SSM_STATE, 1), pltpu.roll(h, SSM_STATE, 1))
        h = a1 * h + a2 * swapped + s[j * nseq:(j + 1) * nseq, :]
    h_scr[...] = h
    y = _dot(xb, t_ref[...]) + _dot(hp_scr[...].astype(BF16), v_ref[...])
    if L > 1:
        for t in range(L):
            y_scr[pl.ds(t, n, stride=L), :] = y[:, t * LANES:(t + 1) * LANES]
        ytok = y_scr[...]
    else:
        ytok = y
    ytok = ytok + d_ref[...] * u_ref[...]
    g = _gelu_exact(ytok)
    z = _dot(g.astype(BF16), gw_ref[...]) + gb_ref[...]
    y_ref[...] = (g * jax.nn.sigmoid(z)).astype(y_ref.dtype)

    @pl.when(i == pl.num_programs(1) - 1)
    def _():
        hout_ref[...] = h


def s5_mixer(u_rows, h0_re, h0_im, tables, ssm_d, glu_w, glu_b, *, nseq, chunk_len, chunks_per_tile,
             precise_state):
    tmat, wmat, vmat, a1, a2 = tables
    mrows, width = u_rows.shape
    nblk = width // LANES
    gb = GROUPS_PER_LANE_BLOCK
    L = chunk_len
    rows_tile = chunks_per_tile * nseq * L
    nt = mrows // rows_tile
    g_total = h0_re.shape[1]
    n = h0_re.shape[2]
    h0 = jnp.stack([h0_re.astype(F32), h0_im.astype(F32)], axis=2)
    h0 = h0.reshape(nseq, nblk, gb * 2 * n).transpose(1, 0, 2)
    drow = ssm_d.astype(F32).reshape(nblk, 1, LANES)
    gwb = jnp.einsum('qgce,gh->qgche', glu_w.astype(F32).reshape(nblk, gb, SSM_GROUP, SSM_GROUP),
                     jnp.eye(gb, dtype=F32)).reshape(nblk, LANES, LANES).astype(BF16)
    gbrow = glu_b.astype(F32).reshape(nblk, 1, LANES)
    kl = L * LANES
    sl = STATE_LANES
    blk = lambda q, i: (q, 0, 0)
    w_hi = wmat.astype(BF16)
    if precise_state:
        w_parts = jnp.stack([w_hi, (wmat - w_hi.astype(F32)).astype(BF16)], axis=1)
    else:
        w_parts = w_hi[:, None]
    nparts = w_parts.shape[1]
    y, hout = pl.pallas_call(
        functools.partial(_s5_kernel, chunk_len=L, nseq=nseq, precise_state=precise_state),
        grid=(nblk, nt),
        in_specs=[pl.BlockSpec((rows_tile, LANES), lambda q, i: (i, q)),
                  pl.BlockSpec((None, nseq, sl), blk),
                  pl.BlockSpec((None, kl, kl), blk),
                  pl.BlockSpec((None, nparts, kl, sl), lambda q, i: (q, 0, 0, 0)),
                  pl.BlockSpec((None, sl, kl), blk),
                  pl.BlockSpec((None, 1, sl), blk),
                  pl.BlockSpec((None, 1, sl), blk),
                  pl.BlockSpec((None, 1, LANES), blk),
                  pl.BlockSpec((None, LANES, LANES), blk),
                  pl.BlockSpec((None, 1, LANES), blk)],
        out_specs=[pl.BlockSpec((rows_tile, LANES), lambda q, i: (i, q)),
                   pl.BlockSpec((None, nseq, sl), blk)],
        out_shape=[jax.ShapeDtypeStruct((mrows, width), BF16),
                   jax.ShapeDtypeStruct((nblk, nseq, sl), F32)],
        scratch_shapes=[pltpu.VMEM((nseq, sl), F32),
                        pltpu.VMEM((chunks_per_tile * nseq, sl), F32),
                        pltpu.VMEM((rows_tile, LANES), F32)],
        compiler_params=_cparams("parallel", "arbitrary"),
        name="s5_mixer",
    )(u_rows, h0, tmat.astype(BF16), w_parts, vmat.astype(BF16), a1, a2, drow, gwb, gbrow)
    hout = hout.transpose(1, 0, 2).reshape(nseq, g_total, 2, n)
    return y, hout[:, :, 0, :], hout[:, :, 1, :]


def _flat_shift(x, s):
    rows, lanes = x.shape
    row = lax.broadcasted_iota(jnp.int32, x.shape, 0)
    if s < lanes:
        lane = lax.broadcasted_iota(jnp.int32, x.shape, 1)
        y = pltpu.roll(x, lanes - s, 1)
        ynext = pltpu.roll(y, rows - 1, 0)
        return jnp.where(lane >= lanes - s, jnp.where(row < rows - 1, ynext, 0.0), y)
    k = s // lanes
    return jnp.where(row < rows - k, pltpu.roll(x, rows - k, 0), 0.0)


def _suffix_sum_exclusive(x):
    e = _flat_shift(x, N_HEADS)
    s = N_HEADS
    while s < x.size:
        e = e + _flat_shift(e, s)
        s *= 2
    return e


def _head_totals(x):
    t = jnp.sum(x, axis=0, keepdims=True)
    s = N_HEADS
    while s < x.shape[1]:
        t = t + pltpu.roll(t, s, 1)
        s *= 2
    return t


def _fox_sample_kernel_flat(pt_ref, q_ref, kn_ref, vn_ref, lfn_ref, *rest, pages_per_step):
    pp = pages_per_step
    k_refs = rest[:pp]
    v_refs = rest[pp:2 * pp]
    lf_refs = rest[2 * pp:3 * pp]
    o_ref, m_ref, l_ref, acc_ref, carry_ref, ccol_ref = rest[3 * pp:]
    g = pl.program_id(1)
    nrow = q_ref.shape[0]
    row_h = lax.broadcasted_iota(jnp.int32, (nrow, LANES), 0) % N_HEADS
    lane = lax.broadcasted_iota(jnp.int32, (nrow, LANES), 1)
    same_head = row_h == lane % N_HEADS

    def new_cumsum():
        c = lfn_ref[...]
        s = N_HEADS
        while s < nrow:
            c = c + pltpu.roll(c, s, 1)
            s *= 2
        return c

    @pl.when(g == 0)
    def _():
        m_ref[...] = jnp.full_like(m_ref, NEG_INF)
        l_ref[...] = jnp.zeros_like(l_ref)
        acc_ref[...] = jnp.zeros_like(acc_ref)
        carry_ref[...] = jnp.zeros_like(carry_ref)
        own = lax.broadcasted_iota(jnp.int32, (nrow, LANES), 0) == lane
        ccol_ref[...] = jnp.sum(jnp.where(own, new_cumsum(), 0.0), axis=1, keepdims=True)

    def update(blocks, v_list):
        m_old = m_ref[...]
        mx = None
        for grp in blocks:
            for b in grp:
                mx = b if mx is None else jnp.maximum(mx, b)
        m_new = jnp.maximum(m_old, jnp.max(mx, axis=1, keepdims=True))
        alpha = jnp.exp(m_old - m_new)
        psum = jnp.zeros((nrow, LANES), F32)
        pv = jnp.zeros(acc_ref.shape, F32)
        for grp, v in zip(blocks, v_list):
            ps = [jnp.exp(b - m_new) for b in grp]
            for p in ps:
                psum = psum + p
            pcat = ps[0] if len(ps) == 1 else jnp.concatenate(ps, axis=1)
            pv = pv + _dot(pcat.astype(BF16), v)
        l_ref[...] = alpha * l_ref[...] + jnp.sum(psum, axis=1, keepdims=True)
        m_ref[...] = m_new
        acc_ref[...] = alpha * acc_ref[...] + pv

    q = q_ref[...]
    ccol = ccol_ref[...]
    carry = carry_ref[...]
    blocks, v_list = [], []
    for pi in range(pp):
        kpage = k_refs[pi][...]
        nkeys = kpage.shape[0] * kpage.shape[1]
        k2 = kpage.reshape(nkeys, kpage.shape[2]).astype(BF16)
        s = _dot_nt(q, k2)
        lf8 = lf_refs[pi][...]
        rs8 = _suffix_sum_exclusive(lf8) + carry
        carry = carry + _head_totals(lf8)
        grp = []
        for r in range(nkeys // LANES):
            sb = s[:, r * LANES:(r + 1) * LANES] + rs8[r:r + 1, :] + ccol
            grp.append(jnp.where(same_head, sb, NEG_INF))
        blocks.append(grp)
        v_list.append(v_refs[pi][...].reshape(nkeys, kpage.shape[2]).astype(BF16))
    carry_ref[...] = carry
    update(blocks, v_list)

    @pl.when(g == pl.num_programs(1) - 1)
    def _():
        s = _dot_nt(q, kn_ref[...])
        s = s + ccol - new_cumsum()
        causal = lane // N_HEADS <= lax.broadcasted_iota(jnp.int32, (nrow, LANES), 0) // N_HEADS
        s = jnp.where(same_head & causal & (lane < nrow), s, NEG_INF)
        update([[s]], [vn_ref[...]])
        o_ref[...] = (acc_ref[...] / l_ref[...]).astype(o_ref.dtype)


def _lane_shift_down(x, s):
    lanes = x.shape[1]
    lane = lax.broadcasted_iota(jnp.int32, x.shape, 1)
    return jnp.where(lane < lanes - s, pltpu.roll(x, lanes - s, 1), 0.0)


def _lane_shift_up(x, s):
    lane = lax.broadcasted_iota(jnp.int32, x.shape, 1)
    return jnp.where(lane >= s, pltpu.roll(x, s, 1), 0.0)


def _fox_sample_kernel(pt_ref, q_ref, knt_ref, vnt_ref, lfnt_ref, *rest, pages_per_step, n_new):
    pp = pages_per_step
    k_refs = rest[:pp]
    v_refs = rest[pp:2 * pp]
    lf_refs = rest[2 * pp:3 * pp]
    o_ref, m_ref, l_ref, acc_ref, carry_ref, qbd_ref, ccol_ref = rest[3 * pp:]
    g = pl.program_id(1)
    nrow = n_new * N_HEADS
    width = q_ref.shape[1]
    page = lf_refs[0].shape[1]
    headmask = (lax.broadcasted_iota(jnp.int32, (N_HEADS, width), 0)
                == lax.broadcasted_iota(jnp.int32, (N_HEADS, width), 1) // HEAD_DIM)
    lane = lax.broadcasted_iota(jnp.int32, (N_HEADS, page), 1)

    def new_cumsum_t():
        c = lfnt_ref[...]
        s = 1
        while s < n_new:
            c = c + _lane_shift_up(c, s)
            s *= 2
        return c

    def tile_rows(x):
        return jnp.concatenate([x] * n_new, axis=0)

    @pl.when(g == 0)
    def _():
        m_ref[...] = jnp.full_like(m_ref, NEG_INF)
        l_ref[...] = jnp.zeros_like(l_ref)
        acc_ref[...] = jnp.zeros_like(acc_ref)
        carry_ref[...] = jnp.zeros_like(carry_ref)
        q = q_ref[...].astype(F32)
        cn = new_cumsum_t()
        for qi in range(n_new):
            qrow = jnp.broadcast_to(q[qi:qi + 1, :], (N_HEADS, width))
            qbd_ref[qi * N_HEADS:(qi + 1) * N_HEADS, :] = jnp.where(headmask, qrow, 0.0).astype(BF16)
            ccol_ref[qi * N_HEADS:(qi + 1) * N_HEADS, :] = jnp.sum(
                jnp.where(lane == qi, cn, 0.0), axis=1, keepdims=True)

    def update(s_list, vt_list):
        m_old = m_ref[...]
        mx = s_list[0]
        for s in s_list[1:]:
            mx = jnp.maximum(mx, s)
        m_new = jnp.maximum(m_old, jnp.max(mx, axis=1, keepdims=True))
        alpha = jnp.exp(m_old - m_new)
        psum = jnp.zeros(s_list[0].shape, F32)
        pv = jnp.zeros(acc_ref.shape, F32)
        for s, vt in zip(s_list, vt_list):
            p = jnp.exp(s - m_new)
            psum = psum + p
            pv = pv + _dot_nt(p.astype(BF16), vt)
        l_ref[...] = alpha * l_ref[...] + jnp.sum(psum, axis=1, keepdims=True)
        m_ref[...] = m_new
        acc_ref[...] = alpha * acc_ref[...] + pv

    qbd = qbd_ref[...]
    ccol = ccol_ref[...]
    carry = carry_ref[...]
    s_list, vt_list = [], []
    for pi in range(pp):
        kt = k_refs[pi][...]
        kt2 = kt.reshape(kt.shape[0] * kt.shape[1], kt.shape[2]).astype(BF16)
        lf = lf_refs[pi][...]
        rs = _lane_shift_down(lf, 1)
        s = 1
        while s < page:
            rs = rs + _lane_shift_down(rs, s)
            s *= 2
        rs = rs + carry
        carry = carry + jnp.sum(lf, axis=1, keepdims=True)
        s_list.append(_dot(qbd, kt2) + tile_rows(rs) + ccol)
        vt = v_refs[pi][...]
        vt_list.append(vt.reshape(vt.shape[0] * vt.shape[1], vt.shape[2]).astype(BF16))
    carry_ref[...] = carry
    update(s_list, vt_list)

    @pl.when(g == pl.num_programs(1) - 1)
    def _():
        s = _dot(qbd, knt_ref[...]) + ccol - tile_rows(new_cumsum_t())
        kj = lax.broadcasted_iota(jnp.int32, (nrow, page), 1)
        qi = lax.broadcasted_iota(jnp.int32, (nrow, page), 0) // N_HEADS
        s = jnp.where(kj <= qi, s, NEG_INF)
        update([s], [vnt_ref[...]])
        o = acc_ref[...] / l_ref[...]
        for q_i in range(n_new):
            blk = o[q_i * N_HEADS:(q_i + 1) * N_HEADS, :]
            o_ref[q_i:q_i + 1, :] = jnp.sum(jnp.where(headmask, blk, 0.0), axis=0,
                                            keepdims=True).astype(o_ref.dtype)


def fox_sample_attention(qb, kb_new, vb_new, logf_new, cache_k, cache_v, cache_logf, page_table,
                         *, pages_per_step):
    nb, n_new, a = qb.shape
    n_pool, page, nh, hd = cache_k.shape
    n_pages = page_table.shape[1]
    pp = pages_per_step
    assert n_new <= page and nh == N_HEADS
    ckt = cache_k.transpose(0, 2, 3, 1)
    cvt = cache_v.transpose(0, 2, 3, 1)
    clt = cache_logf.transpose(0, 2, 1)
    lane_pad = ((0, 0), (0, 0), (0, page - n_new))
    knt = jnp.pad(kb_new.transpose(0, 2, 1), lane_pad)
    vnt = jnp.pad(vb_new.transpose(0, 2, 1), lane_pad)
    lfnt = jnp.pad(logf_new.transpose(0, 2, 1), lane_pad)
    pt = page_table.reshape(-1).astype(jnp.int32)

    def page_map(pi, ndim):
        return lambda b, g, pt_ref: (pt_ref[b * n_pages + (n_pages - 1 - (g * pp + pi))],) + (0,) * (ndim - 1)

    per_b = lambda b, g, pt_ref: (b, 0, 0)
    nrow = n_new * nh
    grid_spec = pltpu.PrefetchScalarGridSpec(
        num_scalar_prefetch=1,
        grid=(nb, n_pages // pp),
        in_specs=[pl.BlockSpec((None, n_new, a), per_b),
                  pl.BlockSpec((None, a, page), per_b),
                  pl.BlockSpec((None, a, page), per_b),
                  pl.BlockSpec((None, nh, page), per_b)]
                 + [pl.BlockSpec((None, nh, hd, page), page_map(pi, 4)) for pi in range(pp)]
                 + [pl.BlockSpec((None, nh, hd, page), page_map(pi, 4)) for pi in range(pp)]
                 + [pl.BlockSpec((None, nh, page), page_map(pi, 3)) for pi in range(pp)],
        out_specs=pl.BlockSpec((None, n_new, a), per_b),
        scratch_shapes=[pltpu.VMEM((nrow, 1), F32), pltpu.VMEM((nrow, 1), F32),
                        pltpu.VMEM((nrow, a), F32), pltpu.VMEM((nh, 1), F32),
                        pltpu.VMEM((nrow, a), BF16), pltpu.VMEM((nrow, 1), F32)],
    )
    return pl.pallas_call(
        functools.partial(_fox_sample_kernel, pages_per_step=pp, n_new=n_new),
        grid_spec=grid_spec,
        out_shape=jax.ShapeDtypeStruct((nb, n_new, a), BF16),
        compiler_params=_cparams("parallel", "arbitrary"),
        name="fox_sample",
    )(pt, qb, knt, vnt, lfnt, *([ckt] * pp), *([cvt] * pp), *([clt] * pp))


def fox_sample_attention_flat(qb, kb_new, vb_new, logf_new, cache_k, cache_v, cache_logf, page_table,
                              *, pages_per_step):
    nb, n_new, a = qb.shape
    n_pool, page, nh, hd = cache_k.shape
    n_pages = page_table.shape[1]
    pp = pages_per_step
    nrow = n_new * nh
    assert page * nh == N_HEADS * LANES and nrow <= LANES
    q32 = qb.reshape(nb, nrow, hd)
    pad = ((0, 0), (0, LANES - nrow), (0, 0))
    kn = jnp.pad(kb_new.reshape(nb, nrow, hd), pad)
    vn = jnp.pad(vb_new.reshape(nb, nrow, hd), pad)
    lfn = jnp.pad(logf_new.reshape(nb, 1, nrow), ((0, 0), (0, 0), (0, LANES - nrow)))
    lf_flat = cache_logf.reshape(n_pool, N_HEADS, LANES)
    pt = page_table.reshape(-1).astype(jnp.int32)

    def page_map(pi, ndim):
        return lambda b, g, pt_ref: (pt_ref[b * n_pages + (n_pages - 1 - (g * pp + pi))],) + (0,) * (ndim - 1)

    per_b = lambda b, g, pt_ref: (b, 0, 0)
    grid_spec = pltpu.PrefetchScalarGridSpec(
        num_scalar_prefetch=1,
        grid=(nb, n_pages // pp),
        in_specs=[pl.BlockSpec((None, nrow, hd), per_b),
                  pl.BlockSpec((None, LANES, hd), per_b),
                  pl.BlockSpec((None, LANES, hd), per_b),
                  pl.BlockSpec((None, 1, LANES), per_b)]
                 + [pl.BlockSpec((None, page, nh, hd), page_map(pi, 4)) for pi in range(pp)]
                 + [pl.BlockSpec((None, page, nh, hd), page_map(pi, 4)) for pi in range(pp)]
                 + [pl.BlockSpec((None, N_HEADS, LANES), page_map(pi, 3)) for pi in range(pp)],
        out_specs=pl.BlockSpec((None, nrow, hd), per_b),
        scratch_shapes=[pltpu.VMEM((nrow, 1), F32), pltpu.VMEM((nrow, 1), F32),
                        pltpu.VMEM((nrow, hd), F32), pltpu.VMEM((1, LANES), F32),
                        pltpu.VMEM((nrow, 1), F32)],
    )
    out = pl.pallas_call(
        functools.partial(_fox_sample_kernel, pages_per_step=pp),
        grid_spec=grid_spec,
        out_shape=jax.ShapeDtypeStruct((nb, nrow, hd), BF16),
        compiler_params=_cparams("parallel", "arbitrary"),
        name="fox_sample",
    )(pt, q32, kn, vn, lfn, *([cache_k] * pp), *([cache_v] * pp), *([lf_flat] * pp))
    return out.reshape(nb, n_new, a)


QROWS_PAD = 16


def _fox_sample_kernel_v1(pt_ref, q_ref, kn_ref, vn_ref, lfn_ref, usup_ref, *rest, pages_per_step, n_new):
    pp = pages_per_step
    k_refs = rest[:pp]
    v_refs = rest[pp:2 * pp]
    lf_refs = rest[2 * pp:3 * pp]
    o_ref, m_ref, l_ref, acc_ref, carry_ref, qbd_ref, ccol_ref = rest[3 * pp:]
    g = pl.program_id(1)
    nrow = n_new * N_HEADS
    width = q_ref.shape[1]

    sub_h = lax.broadcasted_iota(jnp.int32, (N_HEADS, width), 0)
    lane_h = lax.broadcasted_iota(jnp.int32, (N_HEADS, width), 1) // HEAD_DIM
    headmask = sub_h == lane_h
    e_sel = (lax.broadcasted_iota(jnp.int32, (nrow, N_HEADS), 0) % N_HEADS
             == lax.broadcasted_iota(jnp.int32, (nrow, N_HEADS), 1))
    e_f32 = e_sel.astype(F32)
    e_bf = e_sel.astype(BF16)

    def new_cumsum_rows():
        lfn = lfn_ref[...]
        rows, run = [], None
        for t in range(n_new):
            run = lfn[t:t + 1, :] if run is None else run + lfn[t:t + 1, :]
            rows.append(run)
        return rows

    @pl.when(g == 0)
    def _():
        m_ref[...] = jnp.full_like(m_ref, NEG_INF)
        l_ref[...] = jnp.zeros_like(l_ref)
        acc_ref[...] = jnp.zeros_like(acc_ref)
        carry_ref[...] = jnp.zeros_like(carry_ref)
        q = q_ref[...].astype(F32)
        cn = new_cumsum_rows()
        for qi in range(n_new):
            qrow = jnp.broadcast_to(q[qi:qi + 1, :], (N_HEADS, width))
            qbd_ref[qi * N_HEADS:(qi + 1) * N_HEADS, :] = jnp.where(headmask, qrow, 0.0).astype(BF16)
            ccol_ref[qi * N_HEADS:(qi + 1) * N_HEADS, :] = jnp.sum(
                e_f32[:N_HEADS, :] * cn[qi], axis=1, keepdims=True)

    def online_update(s, v_bf):
        m_old = m_ref[...]
        m_new = jnp.maximum(m_old, jnp.max(s, axis=1, keepdims=True))
        alpha = jnp.exp(m_old - m_new)
        p = jnp.exp(s - m_new)
        l_ref[...] = alpha * l_ref[...] + jnp.sum(p, axis=1, keepdims=True)
        m_ref[...] = m_new
        acc_ref[...] = alpha * acc_ref[...] + _dot(p.astype(BF16), v_bf)

    qbd = qbd_ref[...]
    ccol = ccol_ref[...]
    usup = usup_ref[...]
    for pi in range(pp):
        kb = k_refs[pi][...].astype(BF16)
        s = _dot_nt(qbd, kb)
        lf = lf_refs[pi][...]
        hi, mid, lo = _split3(lf)
        rs = _dot(usup, hi) + _dot(usup, mid) + _dot(usup, lo) + carry_ref[...]
        r_hi, r_mid, r_lo = _split3(rs)
        bias = _dot_nt(e_bf, r_hi) + _dot_nt(e_bf, r_mid) + _dot_nt(e_bf, r_lo)
        carry_ref[...] = carry_ref[...] + jnp.sum(lf, axis=0, keepdims=True)
        online_update(s + bias + ccol, v_refs[pi][...].astype(BF16))

    @pl.when(g == pl.num_programs(1) - 1)
    def _():
        cn = new_cumsum_rows()
        s = _dot_nt(qbd, kn_ref[...])
        kj = lax.broadcasted_iota(jnp.int32, (nrow, QROWS_PAD), 1)
        qi = lax.broadcasted_iota(jnp.int32, (nrow, QROWS_PAD), 0) // N_HEADS
        bias = jnp.zeros((nrow, QROWS_PAD), F32)
        for t in range(n_new):
            col = jnp.sum(e_f32 * cn[t], axis=1, keepdims=True)
            bias = jnp.where(kj == t, col, bias)
        s = s + ccol - bias
        s = jnp.where(kj <= qi, s, NEG_INF)
        online_update(s, vn_ref[...])
        o = acc_ref[...] / l_ref[...]
        for q_i in range(n_new):
            blk = o[q_i * N_HEADS:(q_i + 1) * N_HEADS, :]
            o_ref[q_i:q_i + 1, :] = jnp.sum(jnp.where(headmask, blk, 0.0), axis=0,
                                            keepdims=True).astype(o_ref.dtype)


def fox_sample_attention_v1(qb, kb_new, vb_new, logf_new, cache_k, cache_v, cache_logf, page_table,
                            *, pages_per_step):
    nb, n_new, a = qb.shape
    n_pool, page, _, _ = cache_k.shape
    n_pages = page_table.shape[1]
    pp = pages_per_step
    ck = cache_k.reshape(n_pool, page, a)
    cv = cache_v.reshape(n_pool, page, a)
    pad = ((0, 0), (0, QROWS_PAD - n_new), (0, 0))
    kn = jnp.pad(kb_new, pad)
    vn = jnp.pad(vb_new, pad)
    usup = jnp.triu(jnp.ones((page, page), F32), 1).astype(BF16)
    pt = page_table.reshape(-1).astype(jnp.int32)

    def page_map(pi):
        return lambda b, g, pt_ref: (pt_ref[b * n_pages + (n_pages - 1 - (g * pp + pi))], 0, 0)

    per_b = lambda b, g, pt_ref: (b, 0, 0)
    grid_spec = pltpu.PrefetchScalarGridSpec(
        num_scalar_prefetch=1,
        grid=(nb, n_pages // pp),
        in_specs=[pl.BlockSpec((None, n_new, a), per_b),
                  pl.BlockSpec((None, QROWS_PAD, a), per_b),
                  pl.BlockSpec((None, QROWS_PAD, a), per_b),
                  pl.BlockSpec((None, n_new, N_HEADS), per_b),
                  pl.BlockSpec((page, page), lambda b, g, pt_ref: (0, 0))]
                 + [pl.BlockSpec((None, page, a), page_map(pi)) for pi in range(pp)]
                 + [pl.BlockSpec((None, page, a), page_map(pi)) for pi in range(pp)]
                 + [pl.BlockSpec((None, page, N_HEADS), page_map(pi)) for pi in range(pp)],
        out_specs=pl.BlockSpec((None, n_new, a), per_b),
        scratch_shapes=[pltpu.VMEM((n_new * N_HEADS, 1), F32), pltpu.VMEM((n_new * N_HEADS, 1), F32),
                        pltpu.VMEM((n_new * N_HEADS, a), F32), pltpu.VMEM((1, N_HEADS), F32),
                        pltpu.VMEM((n_new * N_HEADS, a), BF16), pltpu.VMEM((n_new * N_HEADS, 1), F32)],
    )
    return pl.pallas_call(
        functools.partial(_fox_sample_kernel, pages_per_step=pp, n_new=n_new),
        grid_spec=grid_spec,
        out_shape=jax.ShapeDtypeStruct((nb, n_new, a), BF16),
        compiler_params=_cparams("parallel", "arbitrary"),
        name="fox_sample",
    )(pt, qb, kn, vn, logf_new, usup, *([ck] * pp), *([cv] * pp), *([cache_logf] * pp))


def _layer_norm(x, g, b):
    mu = jnp.mean(x, axis=-1, keepdims=True)
    xc = x - mu
    var = jnp.mean(xc * xc, axis=-1, keepdims=True)
    return xc * lax.rsqrt(var + LN_EPS) * g + b


def _outproj_router_kernel(*refs, n_experts, aliased, n_tiles):
    if aliased:
        refs = refs[1:]
    ins, (h_ref, idx_ref, w_ref, rank_ref, cnt_out_ref, cnt_scr) = refs[:11], refs[11:]
    cnt_in_ref = ins[10]
    i = pl.program_id(0)

    @pl.when(i == 0)
    def _():
        cnt_scr[...] = cnt_in_ref[...]

    @pl.when(i < n_tiles)
    def _():
        _outproj_router_tile(*ins[:10], h_ref, idx_ref, w_ref, rank_ref, cnt_scr, n_experts=n_experts)

    @pl.when(i >= n_tiles)
    def _():
        h_ref[...] = jnp.zeros_like(h_ref)

    cnt_out_ref[...] = cnt_scr[...]


def _outproj_router_tile(att_ref, ssm_ref, x_ref, woa_ref, wos_ref, g_ref, b_ref, rw_ref, rb_ref,
                         tri_ref, h_ref, idx_ref, w_ref, rank_ref, cnt_scr, *, n_experts):
    tm = x_ref.shape[0]
    att = att_ref[...]
    ssm = ssm_ref[...].reshape(tm, ssm_ref.shape[-1])
    mixed = _dot(att, woa_ref[...]) + _dot(ssm, wos_ref[...])
    h = _layer_norm(DN_ALPHA * x_ref[...] + mixed, g_ref[...], b_ref[...])
    h_ref[...] = h
    h_hi = h.astype(BF16)
    h_lo = (h - h_hi.astype(F32)).astype(BF16)
    logits = _dot(h_hi, rw_ref[0]) + _dot(h_lo, rw_ref[0]) + _dot(h_hi, rw_ref[1]) + rb_ref[...]
    lane = lax.broadcasted_iota(jnp.int32, logits.shape, 1)
    logits = jnp.where(lane < n_experts, logits, NEG_INF)
    vals, idxs = [], []
    for _ in range(TOP_K):
        mx = jnp.max(logits, axis=1, keepdims=True)
        ix = jnp.min(jnp.where(logits == mx, lane, LANES), axis=1, keepdims=True)
        vals.append(mx)
        idxs.append(ix)
        logits = jnp.where(lane == ix, NEG_INF, logits)
    es = [jnp.exp(v - vals[0]) for v in vals]
    denom = es[0] + es[1] + es[2] + es[3]
    lane8 = lax.broadcasted_iota(jnp.int32, (tm, idx_ref.shape[1]), 1)
    idx_out = jnp.zeros(lane8.shape, jnp.int32)
    w_out = jnp.zeros(lane8.shape, F32)
    onehot = jnp.zeros(logits.shape, F32)
    for k in range(TOP_K):
        onehot = onehot + (lane == idxs[k]).astype(F32)
    before = _dot(tri_ref[...], onehot.astype(BF16)) + cnt_scr[...]
    cnt_scr[...] = cnt_scr[...] + jnp.sum(onehot, axis=0, keepdims=True)
    rank_out = jnp.zeros(lane8.shape, jnp.int32)
    for k in range(TOP_K):
        idx_out = jnp.where(lane8 == k, idxs[k], idx_out)
        w_out = jnp.where(lane8 == k, es[k] / denom, w_out)
        rank_k = jnp.sum(jnp.where(lane == idxs[k], before, 0.0), axis=1, keepdims=True)
        rank_out = jnp.where(lane8 == k, rank_k.astype(jnp.int32), rank_out)
    idx_ref[...] = idx_out
    w_ref[...] = w_out
    rank_ref[...] = rank_out


ROUTE_COLS = 8


def outproj_router(att, ssm, x2, w_out, ln_g, ln_b, router_w, router_b, counts_in, *, tm, m_total,
                   row_block_offset, h_all, ssm_chunked):
    m, d = x2.shape
    a = att.shape[1]
    n_experts = router_w.shape[1]
    nt = m // tm
    woa = w_out[:a].astype(BF16)
    wos = w_out[a:].astype(BF16)
    rw = jnp.pad(router_w.astype(F32), ((0, 0), (0, LANES - n_experts)))
    rw_hi = rw.astype(BF16)
    rw_parts = jnp.stack([rw_hi, (rw - rw_hi.astype(F32)).astype(BF16)])
    rb = jnp.pad(router_b.astype(F32), (0, LANES - n_experts)).reshape(1, LANES)
    aliased = h_all is not None
    n_fill = 0 if aliased else m_total // tm - nt - row_block_offset
    clamp = lambda i: jnp.minimum(i, nt - 1)
    row = lambda i: (clamp(i), 0)
    full = lambda i: (0, 0)
    if ssm_chunked is None:
        ssm_spec = pl.BlockSpec((tm, ssm.shape[-1]), row)
    else:
        nseq, slen, L = ssm_chunked
        tps = slen // tm
        ssm_spec = pl.BlockSpec((tm // L, None, L, ssm.shape[-1]),
                                lambda i: (clamp(i) % tps, clamp(i) // tps, 0, 0))
    in_specs = [pl.BlockSpec((tm, a), row), ssm_spec, pl.BlockSpec((tm, d), row),
                pl.BlockSpec((a, d), full), pl.BlockSpec((w_out.shape[0] - a, d), full),
                pl.BlockSpec((1, d), full), pl.BlockSpec((1, d), full),
                pl.BlockSpec((2, d, LANES), lambda i: (0, 0, 0)), pl.BlockSpec((1, LANES), full),
                pl.BlockSpec((tm, tm), full), pl.BlockSpec((1, LANES), full)]
    tri_strict = jnp.tril(jnp.ones((tm, tm), F32), -1).astype(BF16)
    args = [att, ssm, x2, woa, wos, ln_g.astype(F32).reshape(1, d), ln_b.astype(F32).reshape(1, d), rw_parts, rb,
            tri_strict, counts_in]
    aliases = {}
    if aliased:
        in_specs = [pl.BlockSpec(memory_space=pl.ANY)] + in_specs
        args = [h_all] + args
        aliases = {0: 0}
    return pl.pallas_call(
        functools.partial(_outproj_router_kernel, n_experts=n_experts, aliased=aliased, n_tiles=nt),
        grid=(nt + n_fill,),
        in_specs=in_specs,
        out_specs=[pl.BlockSpec((tm, d), lambda i: (i + row_block_offset, 0)),
                   pl.BlockSpec((tm, ROUTE_COLS), row), pl.BlockSpec((tm, ROUTE_COLS), row),
                   pl.BlockSpec((tm, ROUTE_COLS), row), pl.BlockSpec((1, LANES), full)],
        out_shape=[jax.ShapeDtypeStruct((m_total, d), F32),
                   jax.ShapeDtypeStruct((m, ROUTE_COLS), jnp.int32),
                   jax.ShapeDtypeStruct((m, ROUTE_COLS), F32),
                   jax.ShapeDtypeStruct((m, ROUTE_COLS), jnp.int32),
                   jax.ShapeDtypeStruct((1, LANES), F32)],
        input_output_aliases=aliases,
        scratch_shapes=[pltpu.VMEM((1, LANES), F32)],
        compiler_params=_cparams("arbitrary"),
        name="outproj_router",
    )(*args)


DMA_UNROLL = 8


def _for_each_row(n_rows, fn):
    def body(j, carry):
        for u in range(DMA_UNROLL):
            fn(j * DMA_UNROLL + u, u)
        return carry
    lax.fori_loop(0, n_rows // DMA_UNROLL, body, 0)


def _scatter_rows_kernel(base_ref, ztile_ref, idx_ref, rank_ref, h_ref, zeros_ref, xs_ref, sem, zsem, *, tile):
    n_assign = idx_ref.shape[1]
    n_experts = base_ref.shape[0]

    @pl.when(pl.program_id(0) == 0)
    def _():
        def zero_copy(e):
            row0 = pl.multiple_of(ztile_ref[e], tile)
            return pltpu.make_async_copy(zeros_ref, xs_ref.at[pl.ds(row0, tile), :], zsem)
        for e in range(n_experts):
            zero_copy(e).start()
        for e in range(n_experts):
            zero_copy(e).wait()

    def row_copy(a, dst_row):
        tok = lax.shift_right_logical(a, 2)
        return pltpu.make_async_copy(h_ref.at[pl.ds(tok, 1), :], xs_ref.at[pl.ds(dst_row, 1), :], sem)

    def start(a, u):
        row_copy(a, base_ref[idx_ref[0, a]] + rank_ref[0, a]).start(priority=u % 2)

    def wait(a, u):
        row_copy(a, 0).wait()

    _for_each_row(n_assign, start)
    _for_each_row(n_assign, wait)


def scatter_rows(h_all, idx_flat, rank_flat, base, ztile, *, n_rows_out, tokens_per_tile, expert_tile):
    assert TOP_K == 4
    m, d = h_all.shape
    nt = m // tokens_per_tile
    na = tokens_per_tile * TOP_K
    grid_spec = pltpu.PrefetchScalarGridSpec(
        num_scalar_prefetch=2,
        grid=(nt,),
        in_specs=[pl.BlockSpec((None, 1, na), lambda i, b, z: (i, 0, 0), memory_space=pltpu.SMEM),
                  pl.BlockSpec((None, 1, na), lambda i, b, z: (i, 0, 0), memory_space=pltpu.SMEM),
                  pl.BlockSpec((tokens_per_tile, d), lambda i, b, z: (i, 0)),
                  pl.BlockSpec((expert_tile, d), lambda i, b, z: (0, 0))],
        out_specs=pl.BlockSpec(memory_space=pl.ANY),
        scratch_shapes=[pltpu.SemaphoreType.DMA, pltpu.SemaphoreType.DMA],
    )
    return pl.pallas_call(
        functools.partial(_scatter_rows_kernel, tile=expert_tile),
        grid_spec=grid_spec,
        out_shape=jax.ShapeDtypeStruct((n_rows_out, d), h_all.dtype),
        compiler_params=_cparams("arbitrary"),
        name="moe_scatter",
    )(base, ztile, idx_flat.reshape(nt, 1, na), rank_flat.reshape(nt, 1, na), h_all,
      jnp.zeros((expert_tile, d), h_all.dtype))


def _expert_kernel(te_ref, tv_ref, x_ref, wup_ref, bup_ref, wdn_ref, bdn_ref, y_ref):
    i = pl.program_id(0)
    de = wdn_ref.shape[0]

    @pl.when(tv_ref[i] > 0)
    def _():
        xb = x_ref[...].astype(BF16)
        hcat = _dot(xb, wup_ref[...]) + bup_ref[...]
        x_glu = jnp.minimum(hcat[:, :de], SWIGLU_LIMIT)
        x_lin = jnp.clip(hcat[:, de:], -SWIGLU_LIMIT, SWIGLU_LIMIT)
        act = x_glu * jax.nn.sigmoid(SWIGLU_ALPHA * x_glu) * (x_lin + 1.0)
        y_ref[...] = _dot(act.astype(BF16), wdn_ref[...]) + bdn_ref[...]

    @pl.when(tv_ref[i] == 0)
    def _():
        y_ref[...] = jnp.zeros_like(y_ref)


def expert_mlp(xs, tile_expert, tile_valid, w_up, b_up, w_down, b_down, *, tm):
    npad, d = xs.shape
    n_e, _, d2 = w_up.shape
    de = w_down.shape[1]
    dout = w_down.shape[2]
    nt = npad // tm
    grid_spec = pltpu.PrefetchScalarGridSpec(
        num_scalar_prefetch=2,
        grid=(nt,),
        in_specs=[pl.BlockSpec((tm, d), lambda i, te, tv: (jnp.where(tv[i] > 0, i, te[nt]), 0)),
                  pl.BlockSpec((None, d, d2), lambda i, te, tv: (te[i], 0, 0)),
                  pl.BlockSpec((None, 1, d2), lambda i, te, tv: (te[i], 0, 0)),
                  pl.BlockSpec((None, de, dout), lambda i, te, tv: (te[i], 0, 0)),
                  pl.BlockSpec((None, 1, dout), lambda i, te, tv: (te[i], 0, 0))],
        out_specs=pl.BlockSpec((tm, dout), lambda i, te, tv: (i, 0)),
    )
    return pl.pallas_call(
        _expert_kernel,
        grid_spec=grid_spec,
        out_shape=jax.ShapeDtypeStruct((npad, dout), F32),
        compiler_params=_cparams("arbitrary"),
        name="moe_experts",
    )(tile_expert, tile_valid, xs, w_up.astype(BF16),
      b_up.astype(F32).reshape(n_e, 1, d2), w_down.astype(BF16), b_down.astype(F32).reshape(n_e, 1, dout))


def _combine_ln_kernel(base_ref, idx_ref, rank_ref, h_ref, w_ref, g_ref, b_ref, ys_ref, o_ref, buf, sem):
    tm = h_ref.shape[0]
    n_assign = idx_ref.shape[1]

    def row_copy(a, src_row):
        dst = lax.bitwise_and(a, TOP_K - 1) * tm + lax.shift_right_logical(a, 2)
        return pltpu.make_async_copy(ys_ref.at[pl.ds(src_row, 1), :], buf.at[pl.ds(dst, 1), :], sem)

    def start(a, u):
        row_copy(a, base_ref[idx_ref[0, a]] + rank_ref[0, a]).start(priority=u % 2)

    def wait(a, u):
        row_copy(a, 0).wait()

    _for_each_row(n_assign, start)
    _for_each_row(n_assign, wait)
    w = w_ref[...]
    moe = w[:, 0:1] * buf[0:tm, :]
    for k in range(1, TOP_K):
        moe = moe + w[:, k:k + 1] * buf[k * tm:(k + 1) * tm, :]
    o_ref[...] = _layer_norm(DN_ALPHA * h_ref[...] + moe, g_ref[...], b_ref[...])


def combine_ln(h_all, ys, idx_flat, rank_flat, top_w, base, ln_g, ln_b, *, tm, m, row_block_offset):
    assert TOP_K == 4
    d = h_all.shape[1]
    nt = m // tm
    na = tm * TOP_K
    grid_spec = pltpu.PrefetchScalarGridSpec(
        num_scalar_prefetch=1,
        grid=(nt,),
        in_specs=[pl.BlockSpec((None, 1, na), lambda i, b: (i, 0, 0), memory_space=pltpu.SMEM),
                  pl.BlockSpec((None, 1, na), lambda i, b: (i, 0, 0), memory_space=pltpu.SMEM),
                  pl.BlockSpec((tm, d), lambda i, b: (i + row_block_offset, 0)),
                  pl.BlockSpec((tm, ROUTE_COLS), lambda i, b: (i, 0)),
                  pl.BlockSpec((1, d), lambda i, b: (0, 0)),
                  pl.BlockSpec((1, d), lambda i, b: (0, 0)),
                  pl.BlockSpec(memory_space=pl.ANY)],
        out_specs=pl.BlockSpec((tm, d), lambda i, b: (i, 0)),
        scratch_shapes=[pltpu.VMEM((TOP_K * tm, d), F32), pltpu.SemaphoreType.DMA],
    )
    return pl.pallas_call(
        _combine_ln_kernel,
        grid_spec=grid_spec,
        out_shape=jax.ShapeDtypeStruct((m, d), F32),
        compiler_params=_cparams("arbitrary"),
        name="moe_combine_ln",
    )(base, idx_flat.reshape(nt, 1, na), rank_flat.reshape(nt, 1, na), h_all, top_w,
      ln_g.astype(F32).reshape(1, d), ln_b.astype(F32).reshape(1, d), ys)


def _tile_plan(counts, n_assign, tm):
    n_experts = counts.shape[0]
    n_tiles = n_assign // tm + n_experts
    ntile_e = (counts + tm - 1) // tm
    tile_end = jnp.cumsum(ntile_e)
    tile_start = tile_end - ntile_e
    tile_ids = jnp.arange(n_tiles, dtype=jnp.int32)
    tile_expert = jnp.minimum((tile_ids[:, None] >= tile_end[None, :]).sum(axis=1), n_experts - 1)
    n_used = tile_end[-1]
    tile_valid = (tile_ids < n_used).astype(jnp.int32)
    tile_expert = jnp.concatenate([tile_expert, jnp.maximum(n_used - 1, 0)[None]]).astype(jnp.int32)
    base = (tile_start * tm).astype(jnp.int32)
    ztile = jnp.where(ntile_e > 0, (tile_end - 1) * tm, (n_tiles - 1) * tm).astype(jnp.int32)
    return n_tiles, tile_expert, tile_valid, base, ztile


S5_CHUNK = 16
TOKEN_TILE = 512
ATTN_TILE = 1024
S5_CHUNKS_PER_TILE = 16
PAGES_PER_STEP = 8
EXPERT_TILE = 512
SCATTER_TILE = 512
COMBINE_TILE = 256


def kernel(x_prompt, x_sample, cache_k, cache_v, cache_logf, state_ssm_re, state_ssm_im, page_table, w_in, b_forget, ssm_a_re, ssm_a_im, ssm_log_dt, ssm_b_re, ssm_b_im, ssm_c_re, ssm_c_im, ssm_d, ssm_glu_w, ssm_glu_b, w_out, ln1_g, ln1_b, router_w, router_b, w_up, b_up, w_down, b_down, ln2_g, ln2_b):
    bsz, slen, d = x_prompt.shape
    dec_b, dec_t, _ = x_sample.shape
    m_p = bsz * slen
    m_s = dec_b * dec_t
    m_all = m_p + m_s
    n_groups, n_state = ssm_a_re.shape
    n_experts = router_w.shape[1]
    tm = TOKEN_TILE
    assert m_s == tm and slen % tm == 0 and m_p % tm == 0
    assert (m_all * TOP_K) % EXPERT_TILE == 0 and m_p % COMBINE_TILE == 0 and m_s % COMBINE_TILE == 0

    qb, k_p, v_p, kb, vb, logf_p, u_p = in_projection(x_prompt, w_in, b_forget, tm=tm, chunk_len=S5_CHUNK,
                                                      aug=True)
    att_p = fox_prompt_attention(qb, kb, vb, bsz, slen, tq=ATTN_TILE)
    tables_p = _s5_tables(ssm_a_re, ssm_a_im, ssm_log_dt, ssm_b_re, ssm_b_im, ssm_c_re, ssm_c_im, S5_CHUNK)
    zeros_state = jnp.zeros((bsz, n_groups, n_state), F32)
    ssm_p, hre_p, him_p = s5_mixer(u_p.reshape(m_p, u_p.shape[-1]), zeros_state, zeros_state, tables_p,
                                   ssm_d, ssm_glu_w, ssm_glu_b, nseq=bsz, chunk_len=S5_CHUNK,
                                   chunks_per_tile=S5_CHUNKS_PER_TILE, precise_state=False)

    qs, k_s, v_s, ksb, vsb, logf_s, u_s = in_projection(x_sample.reshape(1, m_s, d), w_in, b_forget,
                                                         tm=tm, chunk_len=None, aug=False)
    a = ATTN_WIDTH
    att_s = fox_sample_attention(qs.reshape(dec_b, dec_t, a), ksb.reshape(dec_b, dec_t, a),
                                 vsb.reshape(dec_b, dec_t, a), logf_s.reshape(dec_b, dec_t, N_HEADS),
                                 cache_k, cache_v, cache_logf, page_table, pages_per_step=PAGES_PER_STEP)
    tables_s = _s5_tables(ssm_a_re, ssm_a_im, ssm_log_dt, ssm_b_re, ssm_b_im, ssm_c_re, ssm_c_im, dec_t)
    ssm_s, hre_s, him_s = s5_mixer(u_s, state_ssm_re, state_ssm_im, tables_s, ssm_d, ssm_glu_w, ssm_glu_b,
                                   nseq=dec_b, chunk_len=dec_t, chunks_per_tile=1, precise_state=True)

    ssm_p4 = ssm_p.reshape(slen // S5_CHUNK, bsz, S5_CHUNK, ssm_p.shape[-1])
    zero_counts = jnp.zeros((1, LANES), F32)
    h_all, idx_p, w_p, rank_p, cnt_p = outproj_router(
        att_p, ssm_p4, x_prompt.reshape(m_p, d), w_out, ln1_g, ln1_b, router_w, router_b, zero_counts,
        tm=tm, m_total=m_all, row_block_offset=0, h_all=None, ssm_chunked=(bsz, slen, S5_CHUNK))
    h_all, idx_s, w_s, rank_s, cnt_all = outproj_router(
        att_s.reshape(m_s, a), ssm_s, x_sample.reshape(m_s, d), w_out, ln1_g, ln1_b, router_w, router_b, cnt_p,
        tm=tm, m_total=m_all, row_block_offset=m_p // tm, h_all=h_all, ssm_chunked=None)

    counts = cnt_all[0, :n_experts].astype(jnp.int32)
    n_tiles, tile_expert, tile_valid, base, ztile = _tile_plan(counts, m_all * TOP_K, EXPERT_TILE)
    flat = lambda t: t[:, :TOP_K].reshape(-1)
    idx_pf, idx_sf, rank_pf, rank_sf = flat(idx_p), flat(idx_s), flat(rank_p), flat(rank_s)
    xs = scatter_rows(h_all, jnp.concatenate([idx_pf, idx_sf]), jnp.concatenate([rank_pf, rank_sf]), base, ztile,
                      n_rows_out=n_tiles * EXPERT_TILE, tokens_per_tile=SCATTER_TILE, expert_tile=EXPERT_TILE)
    ys = expert_mlp(xs, tile_expert, tile_valid, w_up, b_up, w_down, b_down, tm=EXPERT_TILE)
    y_p = combine_ln(h_all, ys, idx_pf, rank_pf, w_p, base, ln2_g, ln2_b, tm=COMBINE_TILE, m=m_p,
                     row_block_offset=0)
    y_s = combine_ln(h_all, ys, idx_sf, rank_sf, w_s, base, ln2_g, ln2_b, tm=COMBINE_TILE, m=m_s,
                     row_block_offset=m_p // COMBINE_TILE)

    hd = (N_HEADS, HEAD_DIM)
    return (y_p.reshape(bsz, slen, d), y_s.reshape(dec_b, dec_t, d),
            k_p.reshape(bsz, slen, *hd), v_p.reshape(bsz, slen, *hd),
            logf_p.reshape(bsz, slen, N_HEADS).astype(cache_logf.dtype),
            hre_p.astype(state_ssm_re.dtype), him_p.astype(state_ssm_im.dtype),
            k_s.reshape(dec_b, dec_t, *hd), v_s.reshape(dec_b, dec_t, *hd),
            logf_s.reshape(dec_b, dec_t, N_HEADS).astype(cache_logf.dtype),
            hre_s.astype(state_ssm_re.dtype), him_s.astype(state_ssm_im.dtype))
```

```python
import functools
import math

import jax
import jax.numpy as jnp
from jax import lax
from jax.experimental import pallas as pl
from jax.experimental.pallas import tpu as pltpu

N_HEADS = 8
HEAD_DIM = 64
ATTN_WIDTH = N_HEADS * HEAD_DIM
SSM_GROUP = 16
SSM_STATE = 64
TOP_K = 4
SWIGLU_LIMIT = 7.0
SWIGLU_ALPHA = 1.702
LN_EPS = 1e-5
DEPTH = 1
DN_ALPHA = (2.0 * DEPTH) ** 0.25
ATTN_SCALE = HEAD_DIM ** -0.5

LANES = 128
GROUPS_PER_LANE_BLOCK = LANES // SSM_GROUP
STATE_LANES = GROUPS_PER_LANE_BLOCK * 2 * SSM_STATE
VMEM_LIMIT_BYTES = 56 * 1024 * 1024

BF16 = jnp.bfloat16
F32 = jnp.float32
NEG_INF = float("-inf")


def _cparams(*sem):
    return pltpu.CompilerParams(dimension_semantics=sem, vmem_limit_bytes=VMEM_LIMIT_BYTES)


def _split3(x):
    hi = x.astype(BF16)
    r1 = x - hi.astype(F32)
    mid = r1.astype(BF16)
    lo = (r1 - mid.astype(F32)).astype(BF16)
    return hi, mid, lo


def _dot(a, b):
    return jnp.dot(a, b, preferred_element_type=F32)


def _dot_nt(a, b):
    return lax.dot_general(a, b, (((1,), (1,)), ((), ())), preferred_element_type=F32)


LOG2E = math.log2(math.e)
AUG_C_LANE = HEAD_DIM
AUG_ONE_LANE = HEAD_DIM + 3


def _aug_tables():
    e = jnp.zeros((LANES, 2 * N_HEADS * LANES), F32)
    ones = jnp.zeros((1, 2 * N_HEADS * LANES), F32)
    koff = N_HEADS * LANES
    for h in range(N_HEADS):
        for p in range(3):
            src = p * N_HEADS + h
            e = e.at[src, h * LANES + AUG_C_LANE + p].set(1.0)
            e = e.at[src, koff + h * LANES + AUG_ONE_LANE + p].set(-1.0)
            ones = ones.at[0, h * LANES + AUG_ONE_LANE + p].set(1.0)
            ones = ones.at[0, koff + h * LANES + AUG_C_LANE + p].set(1.0)
    return e.astype(BF16), ones


def _spread_heads(x):
    low = lax.broadcasted_iota(jnp.int32, (1, LANES), 1) < HEAD_DIM
    blocks = []
    for j in range(x.shape[1] // LANES):
        pair = x[:, j * LANES:(j + 1) * LANES]
        blocks.append(jnp.where(low, pair, 0.0))
        blocks.append(jnp.where(low, pltpu.roll(pair, HEAD_DIM, 1), 0.0))
    return jnp.concatenate(blocks, axis=1)


def _inproj_kernel(x_ref, wqkv_ref, wf_ref, wu_ref, bf_ref, *rest, chunk_len, aug):
    if aug:
        tri_ref, e_ref, ones_ref, q_ref, k_ref, v_ref, kb_ref, vb_ref, logf_ref, u_ref, carry_ref = rest
    else:
        q_ref, k_ref, v_ref, kb_ref, vb_ref, logf_ref, u_ref = rest
    xb = x_ref[...].astype(BF16)
    qkv = _dot(xb, wqkv_ref[...])
    a = ATTN_WIDTH
    q = qkv[:, :a]
    k = qkv[:, a:2 * a]
    v = qkv[:, 2 * a:]
    k_ref[...] = k
    v_ref[...] = v
    vb_ref[...] = v.astype(BF16)
    f = _dot(xb, wf_ref[...]) + bf_ref[...]
    logf = jnp.minimum(f, 0.0) - jnp.log1p(jnp.exp(-jnp.abs(f)))
    logf_ref[...] = logf[:, :N_HEADS]

    if aug:
        @pl.when(pl.program_id(1) == 0)
        def _():
            carry_ref[...] = jnp.zeros_like(carry_ref)
        hi, mid, lo = _split3(logf)
        tri = tri_ref[...]
        c = _dot(tri, hi) + _dot(tri, mid) + _dot(tri, lo) + carry_ref[...]
        carry_ref[...] = c[-1:, :]
        c_hi, c_mid, c_lo = _split3(c * LOG2E)
        lane = lax.broadcasted_iota(jnp.int32, (1, LANES), 1)
        pieces = jnp.where(lane < N_HEADS, c_hi.astype(F32),
                           jnp.where(lane < 2 * N_HEADS, pltpu.roll(c_mid.astype(F32), N_HEADS, 1),
                                     jnp.where(lane < 3 * N_HEADS, pltpu.roll(c_lo.astype(F32), 2 * N_HEADS, 1),
                                               0.0)))
        extra = _dot(pieces.astype(BF16), e_ref[...]) + ones_ref[...]
        half = N_HEADS * LANES
        q_ref[...] = (_spread_heads(q) + extra[:, :half]).astype(BF16)
        kb_ref[...] = (_spread_heads(k) + extra[:, half:]).astype(BF16)
    else:
        q_ref[...] = q.astype(BF16)
        kb_ref[...] = k.astype(BF16)

    u = _dot(xb, wu_ref[...])
    if chunk_len is None:
        u_ref[...] = u
    else:
        u_ref[...] = u.reshape(u.shape[0] // chunk_len, chunk_len, u.shape[1])


def in_projection(x, w_in, b_forget, *, tm, chunk_len, aug):
    nseq, slen, d = x.shape
    m = nseq * slen
    nt = slen // tm
    a = ATTN_WIDTH
    x2 = x.reshape(m, d)
    q_scale = ATTN_SCALE * LOG2E if aug else ATTN_SCALE
    col_scale = jnp.concatenate([jnp.full((a,), q_scale, F32), jnp.ones((2 * a,), F32)])
    wqkv = (w_in[:, :3 * a] * col_scale).astype(BF16)
    wf = jnp.pad(w_in[:, 3 * a:3 * a + N_HEADS], ((0, 0), (0, LANES - N_HEADS))).astype(BF16)
    wu = w_in[:, 3 * a + N_HEADS:].astype(BF16)
    ssm_w = wu.shape[1]
    bf = jnp.pad(b_forget.astype(F32), (0, LANES - N_HEADS)).reshape(1, LANES)

    row = lambda b, i: (b * nt + i, 0)
    full = lambda b, i: (0, 0)
    tok = lambda w, dt: jax.ShapeDtypeStruct((m, w), dt)
    if chunk_len is None:
        u_shape = tok(ssm_w, F32)
        u_spec = pl.BlockSpec((tm, ssm_w), row)
    else:
        u_shape = jax.ShapeDtypeStruct((slen // chunk_len, nseq, chunk_len, ssm_w), F32)
        u_spec = pl.BlockSpec((tm // chunk_len, None, chunk_len, ssm_w), lambda b, i: (i, b, 0, 0))
    in_specs = [pl.BlockSpec((tm, d), row),
                pl.BlockSpec((d, 3 * a), full),
                pl.BlockSpec((d, LANES), full),
                pl.BlockSpec((d, ssm_w), full),
                pl.BlockSpec((1, LANES), full)]
    args = [x2, wqkv, wf, wu, bf]
    qk_w = a
    scratch = []
    if aug:
        e_mat, ones_row = _aug_tables()
        qk_w = N_HEADS * LANES
        in_specs += [pl.BlockSpec((tm, tm), full), pl.BlockSpec((LANES, 2 * qk_w), full),
                     pl.BlockSpec((1, 2 * qk_w), full)]
        args += [jnp.tril(jnp.ones((tm, tm), F32)).astype(BF16), e_mat, ones_row]
        scratch = [pltpu.VMEM((1, LANES), F32)]
    return pl.pallas_call(
        functools.partial(_inproj_kernel, chunk_len=chunk_len, aug=aug),
        grid=(nseq, nt),
        in_specs=in_specs,
        out_specs=[pl.BlockSpec((tm, qk_w), row), pl.BlockSpec((tm, a), row), pl.BlockSpec((tm, a), row),
                   pl.BlockSpec((tm, qk_w), row), pl.BlockSpec((tm, a), row),
                   pl.BlockSpec((tm, N_HEADS), row), u_spec],
        out_shape=[tok(qk_w, BF16), tok(a, F32), tok(a, F32), tok(qk_w, BF16), tok(a, BF16),
                   tok(N_HEADS, F32), u_shape],
        scratch_shapes=scratch,
        compiler_params=_cparams("parallel", "arbitrary"),
        name="in_projection",
    )(*args)


def _fox_prompt_kernel(q_ref, k_ref, v_ref, o_ref, m_ref, l_ref, acc_ref):
    i = pl.program_id(1)
    kk = pl.program_id(2)
    nk = pl.num_programs(2)
    tq = q_ref.shape[0]
    tk = k_ref.shape[0]

    @pl.when(kk == 0)
    def _():
        m_ref[...] = jnp.full_like(m_ref, NEG_INF)
        l_ref[...] = jnp.zeros_like(l_ref)
        acc_ref[...] = jnp.zeros_like(acc_ref)

    def step(masked):
        lane = lax.broadcasted_iota(jnp.int32, (1, LANES), 1)
        low = lane < HEAD_DIM
        if masked:
            qpos = lax.broadcasted_iota(jnp.int32, (tq, tk), 0)
            kpos = lax.broadcasted_iota(jnp.int32, (tq, tk), 1)
            causal = kpos <= qpos
        for blk in range(N_HEADS // 2):
            sl = slice(blk * LANES, (blk + 1) * LANES)
            v = v_ref[:, sl]
            pvs, alphas = [], []
            for half in range(2):
                h = 2 * blk + half
                hs = slice(h * LANES, (h + 1) * LANES)
                s = _dot_nt(q_ref[:, hs], k_ref[:, hs])
                if masked:
                    s = jnp.where(causal, s, NEG_INF)
                m_old = m_ref[:, h:h + 1]
                m_new = jnp.maximum(m_old, jnp.max(s, axis=1, keepdims=True))
                alpha = jnp.exp2(m_old - m_new)
                p = jnp.exp2(s - m_new)
                l_ref[:, h:h + 1] = alpha * l_ref[:, h:h + 1] + jnp.sum(p, axis=1, keepdims=True)
                m_ref[:, h:h + 1] = m_new
                pvs.append(_dot(p.astype(BF16), v))
                alphas.append(alpha)
            alpha2 = jnp.where(low, alphas[0], alphas[1])
            pv2 = jnp.where(low, pvs[0], pvs[1])
            acc_ref[:, sl] = alpha2 * acc_ref[:, sl] + pv2

    @pl.when(kk < i)
    def _():
        step(False)

    @pl.when(kk == i)
    def _():
        step(True)

    @pl.when(kk == nk - 1)
    def _():
        lane = lax.broadcasted_iota(jnp.int32, (1, LANES), 1)
        low = lane < HEAD_DIM
        for blk in range(N_HEADS // 2):
            sl = slice(blk * LANES, (blk + 1) * LANES)
            l2 = jnp.where(low, l_ref[:, 2 * blk:2 * blk + 1], l_ref[:, 2 * blk + 1:2 * blk + 2])
            o_ref[:, sl] = (acc_ref[:, sl] / l2).astype(o_ref.dtype)


def fox_prompt_attention(q_aug, k_aug, vb, nseq, slen, *, tq):
    m = nseq * slen
    nq = slen // tq
    a = ATTN_WIDTH
    aw = q_aug.shape[1]
    qmap = lambda b, i, kk: (b * nq + i, 0)
    kmap = lambda b, i, kk: (b * nq + jnp.minimum(kk, i), 0)
    return pl.pallas_call(
        _fox_prompt_kernel,
        grid=(nseq, nq, nq),
        in_specs=[pl.BlockSpec((tq, aw), qmap),
                  pl.BlockSpec((tq, aw), kmap),
                  pl.BlockSpec((tq, a), kmap)],
        out_specs=pl.BlockSpec((tq, a), qmap),
        out_shape=jax.ShapeDtypeStruct((m, a), BF16),
        scratch_shapes=[pltpu.VMEM((tq, N_HEADS), F32), pltpu.VMEM((tq, N_HEADS), F32),
                        pltpu.VMEM((tq, a), F32)],
        compiler_params=_cparams("parallel", "parallel", "arbitrary"),
        name="fox_prompt",
    )(q_aug, k_aug, vb)


def _s5_tables(a_re, a_im, log_dt, b_re, b_im, c_re, c_im, chunk_len):
    g, n = a_re.shape
    cdim = b_re.shape[2]
    nblk = g // GROUPS_PER_LANE_BLOCK
    gb = GROUPS_PER_LANE_BLOCK
    a_re = a_re.astype(F32)
    a_im = a_im.astype(F32)
    b_re = b_re.astype(F32)
    b_im = b_im.astype(F32)
    c_re = c_re.astype(F32)
    c_im = c_im.astype(F32)
    dt = jnp.exp(log_dt.astype(F32))[:, None]
    mag = jnp.exp(a_re * dt)
    ar = mag * jnp.cos(a_im * dt)
    ai = mag * jnp.sin(a_im * dt)
    den = a_re * a_re + a_im * a_im
    cr = ((ar - 1.0) * a_re + ai * a_im) / den
    ci = (ai * a_re - (ar - 1.0) * a_im) / den
    bbr = cr[..., None] * b_re - ci[..., None] * b_im
    bbi = cr[..., None] * b_im + ci[..., None] * b_re
    pr = [jnp.ones_like(ar)]
    pi = [jnp.zeros_like(ai)]
    for _ in range(chunk_len):
        pr_n = pr[-1] * ar - pi[-1] * ai
        pi_n = pr[-1] * ai + pi[-1] * ar
        pr.append(pr_n)
        pi.append(pi_n)
    pr = jnp.stack(pr)
    pi = jnp.stack(pi)
    L = chunk_len
    pbr = pr[:L, :, :, None] * bbr[None] - pi[:L, :, :, None] * bbi[None]
    pbi = pr[:L, :, :, None] * bbi[None] + pi[:L, :, :, None] * bbr[None]
    kmat = jnp.einsum('gon,lgni->lgoi', c_re, pbr) - jnp.einsum('gon,lgni->lgoi', c_im, pbi)
    eye = jnp.eye(gb, dtype=F32)
    bd = jnp.einsum('lqgoi,gh->lqgiho', kmat.reshape(L, nblk, gb, cdim, cdim), eye)
    bd = bd.reshape(L, nblk, gb * cdim, gb * cdim)
    steps = jnp.arange(L)
    shift = (steps[None, None, :] - steps[None, :, None] == steps[:, None, None]).astype(F32)
    tmat = jnp.einsum('lst,lqrc->qsrtc', shift, bd, precision=lax.Precision.HIGHEST)
    tmat = tmat.reshape(nblk, L * gb * cdim, L * gb * cdim)
    wr = pbr[::-1].reshape(L, nblk, gb, n, cdim)
    wi = pbi[::-1].reshape(L, nblk, gb, n, cdim)
    wri = jnp.stack([wr, wi], axis=3)
    wmat = jnp.einsum('sqgrni,gh->qsgihrn', wri, eye).reshape(nblk, L * gb * cdim, gb * 2 * n)
    cpr = c_re[None] * pr[1:, :, None, :] - c_im[None] * pi[1:, :, None, :]
    cpi = c_re[None] * pi[1:, :, None, :] + c_im[None] * pr[1:, :, None, :]
    vri = jnp.stack([cpr, -cpi], axis=1).reshape(L, 2, nblk, gb, cdim, n)
    vmat = jnp.einsum('trqgon,gh->qgrntho', vri, eye).reshape(nblk, gb * 2 * n, L * gb * cdim)
    al_r = pr[L].reshape(nblk, gb, n)
    al_i = pi[L].reshape(nblk, gb, n)
    a1 = jnp.stack([al_r, al_r], axis=2).reshape(nblk, 1, gb * 2 * n)
    a2 = jnp.stack([-al_i, al_i], axis=2).reshape(nblk, 1, gb * 2 * n)
    return tmat, wmat, vmat, a1, a2


def _gelu_exact(y):
    return 0.5 * y * (1.0 + lax.erf(y * (2.0 ** -0.5)))


def _s5_kernel(u_ref, h0_ref, t_ref, w_ref, v_ref, a1_ref, a2_ref, d_ref, gw_ref, gb_ref,
               y_ref, hout_ref, h_scr, hp_scr, y_scr, *, chunk_len, nseq, precise_state):
    i = pl.program_id(1)
    L = chunk_len
    rows = u_ref.shape[0]
    n = rows // L
    nj = n // nseq

    @pl.when(i == 0)
    def _():
        h_scr[...] = h0_ref[...]

    xs = [u_ref[pl.ds(t, n, stride=L), :] for t in range(L)] if L > 1 else [u_ref[...]]
    x32 = jnp.concatenate(xs, axis=1)
    xb = x32.astype(BF16)
    w = w_ref[0]
    s = _dot(xb, w)
    if precise_state:
        xlo = (x32 - xb.astype(F32)).astype(BF16)
        s = s + _dot(xlo, w) + _dot(xb, w_ref[1])
    a1 = a1_ref[...]
    a2 = a2_ref[...]
    h = h_scr[...]
    width = h.shape[1]
    is_re = (lax.broadcasted_iota(jnp.int32, (1, width), 1) % (2 * SSM_STATE)) < SSM_STATE
    for j in range(nj):
        hp_scr[j * nseq:(j + 1) * nseq, :] = h
        swapped = jnp.where(is_re, pltpu.roll(h, width - SSM_STATE, 1), pltpu.roll(h, SSM_STATE, 1))
        h = a1 * h + a2 * swapped + s[j * nseq:(j + 1) * nseq, :]
    h_scr[...] = h
    y = _dot(xb, t_ref[...]) + _dot(hp_scr[...].astype(BF16), v_ref[...])
    if L > 1:
        for t in range(L):
            y_scr[pl.ds(t, n, stride=L), :] = y[:, t * LANES:(t + 1) * LANES]
        ytok = y_scr[...]
    else:
        ytok = y
    ytok = ytok + d_ref[...] * u_ref[...]
    g = _gelu_exact(ytok)
    z = _dot(g.astype(BF16), gw_ref[...]) + gb_ref[...]
    y_ref[...] = (g * jax.nn.sigmoid(z)).astype(y_ref.dtype)

    @pl.when(i == pl.num_programs(1) - 1)
    def _():
        hout_ref[...] = h


def s5_mixer(u_rows, h0_re, h0_im, tables, ssm_d, glu_w, glu_b, *, nseq, chunk_len, chunks_per_tile,
             precise_state):
    tmat, wmat, vmat, a1, a2 = tables
    mrows, width = u_rows.shape
    nblk = width // LANES
    gb = GROUPS_PER_LANE_BLOCK
    L = chunk_len
    rows_tile = chunks_per_tile * nseq * L
    nt = mrows // rows_tile
    g_total = h0_re.shape[1]
    n = h0_re.shape[2]
    h0 = jnp.stack([h0_re.astype(F32), h0_im.astype(F32)], axis=2)
    h0 = h0.reshape(nseq, nblk, gb * 2 * n).transpose(1, 0, 2)
    drow = ssm_d.astype(F32).reshape(nblk, 1, LANES)
    gwb = jnp.einsum('qgce,gh->qgche', glu_w.astype(F32).reshape(nblk, gb, SSM_GROUP, SSM_GROUP),
                     jnp.eye(gb, dtype=F32)).reshape(nblk, LANES, LANES).astype(BF16)
    gbrow = glu_b.astype(F32).reshape(nblk, 1, LANES)
    kl = L * LANES
    sl = STATE_LANES
    blk = lambda q, i: (q, 0, 0)
    w_hi = wmat.astype(BF16)
    if precise_state:
        w_parts = jnp.stack([w_hi, (wmat - w_hi.astype(F32)).astype(BF16)], axis=1)
    else:
        w_parts = w_hi[:, None]
    nparts = w_parts.shape[1]
    y, hout = pl.pallas_call(
        functools.partial(_s5_kernel, chunk_len=L, nseq=nseq, precise_state=precise_state),
        grid=(nblk, nt),
        in_specs=[pl.BlockSpec((rows_tile, LANES), lambda q, i: (i, q)),
                  pl.BlockSpec((None, nseq, sl), blk),
                  pl.BlockSpec((None, kl, kl), blk),
                  pl.BlockSpec((None, nparts, kl, sl), lambda q, i: (q, 0, 0, 0)),
                  pl.BlockSpec((None, sl, kl), blk),
                  pl.BlockSpec((None, 1, sl), blk),
                  pl.BlockSpec((None, 1, sl), blk),
                  pl.BlockSpec((None, 1, LANES), blk),
                  pl.BlockSpec((None, LANES, LANES), blk),
                  pl.BlockSpec((None, 1, LANES), blk)],
        out_specs=[pl.BlockSpec((rows_tile, LANES), lambda q, i: (i, q)),
                   pl.BlockSpec((None, nseq, sl), blk)],
        out_shape=[jax.ShapeDtypeStruct((mrows, width), BF16),
                   jax.ShapeDtypeStruct((nblk, nseq, sl), F32)],
        scratch_shapes=[pltpu.VMEM((nseq, sl), F32),
                        pltpu.VMEM((chunks_per_tile * nseq, sl), F32),
                        pltpu.VMEM((rows_tile, LANES), F32)],
        compiler_params=_cparams("parallel", "arbitrary"),
        name="s5_mixer",
    )(u_rows, h0, tmat.astype(BF16), w_parts, vmat.astype(BF16), a1, a2, drow, gwb, gbrow)
    hout = hout.transpose(1, 0, 2).reshape(nseq, g_total, 2, n)
    return y, hout[:, :, 0, :], hout[:, :, 1, :]


def _flat_shift(x, s):
    rows, lanes = x.shape
    row = lax.broadcasted_iota(jnp.int32, x.shape, 0)
    if s < lanes:
        lane = lax.broadcasted_iota(jnp.int32, x.shape, 1)
        y = pltpu.roll(x, lanes - s, 1)
        ynext = pltpu.roll(y, rows - 1, 0)
        return jnp.where(lane >= lanes - s, jnp.where(row < rows - 1, ynext, 0.0), y)
    k = s // lanes
    return jnp.where(row < rows - k, pltpu.roll(x, rows - k, 0), 0.0)


def _suffix_sum_exclusive(x):
    e = _flat_shift(x, N_HEADS)
    s = N_HEADS
    while s < x.size:
        e = e + _flat_shift(e, s)
        s *= 2
    return e


def _head_totals(x):
    t = jnp.sum(x, axis=0, keepdims=True)
    s = N_HEADS
    while s < x.shape[1]:
        t = t + pltpu.roll(t, s, 1)
        s *= 2
    return t


def _fox_sample_kernel_flat(pt_ref, q_ref, kn_ref, vn_ref, lfn_ref, *rest, pages_per_step):
    pp = pages_per_step
    k_refs = rest[:pp]
    v_refs = rest[pp:2 * pp]
    lf_refs = rest[2 * pp:3 * pp]
    o_ref, m_ref, l_ref, acc_ref, carry_ref, ccol_ref = rest[3 * pp:]
    g = pl.program_id(1)
    nrow = q_ref.shape[0]
    row_h = lax.broadcasted_iota(jnp.int32, (nrow, LANES), 0) % N_HEADS
    lane = lax.broadcasted_iota(jnp.int32, (nrow, LANES), 1)
    same_head = row_h == lane % N_HEADS

    def new_cumsum():
        c = lfn_ref[...]
        s = N_HEADS
        while s < nrow:
            c = c + pltpu.roll(c, s, 1)
            s *= 2
        return c

    @pl.when(g == 0)
    def _():
        m_ref[...] = jnp.full_like(m_ref, NEG_INF)
        l_ref[...] = jnp.zeros_like(l_ref)
        acc_ref[...] = jnp.zeros_like(acc_ref)
        carry_ref[...] = jnp.zeros_like(carry_ref)
        own = lax.broadcasted_iota(jnp.int32, (nrow, LANES), 0) == lane
        ccol_ref[...] = jnp.sum(jnp.where(own, new_cumsum(), 0.0), axis=1, keepdims=True)

    def update(blocks, v_list):
        m_old = m_ref[...]
        mx = None
        for grp in blocks:
            for b in grp:
                mx = b if mx is None else jnp.maximum(mx, b)
        m_new = jnp.maximum(m_old, jnp.max(mx, axis=1, keepdims=True))
        alpha = jnp.exp(m_old - m_new)
        psum = jnp.zeros((nrow, LANES), F32)
        pv = jnp.zeros(acc_ref.shape, F32)
        for grp, v in zip(blocks, v_list):
            ps = [jnp.exp(b - m_new) for b in grp]
            for p in ps:
                psum = psum + p
            pcat = ps[0] if len(ps) == 1 else jnp.concatenate(ps, axis=1)
            pv = pv + _dot(pcat.astype(BF16), v)
        l_ref[...] = alpha * l_ref[...] + jnp.sum(psum, axis=1, keepdims=True)
        m_ref[...] = m_new
        acc_ref[...] = alpha * acc_ref[...] + pv

    q = q_ref[...]
    ccol = ccol_ref[...]
    carry = carry_ref[...]
    blocks, v_list = [], []
    for pi in range(pp):
        kpage = k_refs[pi][...]
        nkeys = kpage.shape[0] * kpage.shape[1]
        k2 = kpage.reshape(nkeys, kpage.shape[2]).astype(BF16)
        s = _dot_nt(q, k2)
        lf8 = lf_refs[pi][...]
        rs8 = _suffix_sum_exclusive(lf8) + carry
        carry = carry + _head_totals(lf8)
        grp = []
        for r in range(nkeys // LANES):
            sb = s[:, r * LANES:(r + 1) * LANES] + rs8[r:r + 1, :] + ccol
            grp.append(jnp.where(same_head, sb, NEG_INF))
        blocks.append(grp)
        v_list.append(v_refs[pi][...].reshape(nkeys, kpage.shape[2]).astype(BF16))
    carry_ref[...] = carry
    update(blocks, v_list)

    @pl.when(g == pl.num_programs(1) - 1)
    def _():
        s = _dot_nt(q, kn_ref[...])
        s = s + ccol - new_cumsum()
        causal = lane // N_HEADS <= lax.broadcasted_iota(jnp.int32, (nrow, LANES), 0) // N_HEADS
        s = jnp.where(same_head & causal & (lane < nrow), s, NEG_INF)
        update([[s]], [vn_ref[...]])
        o_ref[...] = (acc_ref[...] / l_ref[...]).astype(o_ref.dtype)


def _lane_shift_down(x, s):
    lanes = x.shape[1]
    lane = lax.broadcasted_iota(jnp.int32, x.shape, 1)
    return jnp.where(lane < lanes - s, pltpu.roll(x, lanes - s, 1), 0.0)


def _lane_shift_up(x, s):
    lane = lax.broadcasted_iota(jnp.int32, x.shape, 1)
    return jnp.where(lane >= s, pltpu.roll(x, s, 1), 0.0)


def _fox_sample_kernel(pt_ref, q_ref, knt_ref, vnt_ref, lfnt_ref, *rest, pages_per_step, n_new):
    pp = pages_per_step
    k_refs = rest[:pp]
    v_refs = rest[pp:2 * pp]
    lf_refs = rest[2 * pp:3 * pp]
    o_ref, m_ref, l_ref, acc_ref, carry_ref, qbd_ref, ccol_ref = rest[3 * pp:]
    g = pl.program_id(1)
    nrow = n_new * N_HEADS
    width = q_ref.shape[1]
    page = lf_refs[0].shape[1]
    headmask = (lax.broadcasted_iota(jnp.int32, (N_HEADS, width), 0)
                == lax.broadcasted_iota(jnp.int32, (N_HEADS, width), 1) // HEAD_DIM)
    lane = lax.broadcasted_iota(jnp.int32, (N_HEADS, page), 1)

    def new_cumsum_t():
        c = lfnt_ref[...]
        s = 1
        while s < n_new:
            c = c + _lane_shift_up(c, s)
            s *= 2
        return c

    def tile_rows(x):
        return jnp.concatenate([x] * n_new, axis=0)

    @pl.when(g == 0)
    def _():
        m_ref[...] = jnp.full_like(m_ref, NEG_INF)
        l_ref[...] = jnp.zeros_like(l_ref)
        acc_ref[...] = jnp.zeros_like(acc_ref)
        carry_ref[...] = jnp.zeros_like(carry_ref)
        q = q_ref[...].astype(F32)
        cn = new_cumsum_t()
        for qi in range(n_new):
            qrow = jnp.broadcast_to(q[qi:qi + 1, :], (N_HEADS, width))
            qbd_ref[qi * N_HEADS:(qi + 1) * N_HEADS, :] = jnp.where(headmask, qrow, 0.0).astype(BF16)
            ccol_ref[qi * N_HEADS:(qi + 1) * N_HEADS, :] = jnp.sum(
                jnp.where(lane == qi, cn, 0.0), axis=1, keepdims=True)

    def update(s_list, vt_list):
        m_old = m_ref[...]
        mx = s_list[0]
        for s in s_list[1:]:
            mx = jnp.maximum(mx, s)
        m_new = jnp.maximum(m_old, jnp.max(mx, axis=1, keepdims=True))
        alpha = jnp.exp(m_old - m_new)
        psum = jnp.zeros(s_list[0].shape, F32)
        pv = jnp.zeros(acc_ref.shape, F32)
        for s, vt in zip(s_list, vt_list):
            p = jnp.exp(s - m_new)
            psum = psum + p
            pv = pv + _dot_nt(p.astype(BF16), vt)
        l_ref[...] = alpha * l_ref[...] + jnp.sum(psum, axis=1, keepdims=True)
        m_ref[...] = m_new
        acc_ref[...] = alpha * acc_ref[...] + pv

    qbd = qbd_ref[...]
    ccol = ccol_ref[...]
    carry = carry_ref[...]
    s_list, vt_list = [], []
    for pi in range(pp):
        kt = k_refs[pi][...]
        kt2 = kt.reshape(kt.shape[0] * kt.shape[1], kt.shape[2]).astype(BF16)
        lf = lf_refs[pi][...]
        rs = _lane_shift_down(lf, 1)
        s = 1
        while s < page:
            rs = rs + _lane_shift_down(rs, s)
            s *= 2
        rs = rs + carry
        carry = carry + jnp.sum(lf, axis=1, keepdims=True)
        s_list.append(_dot(qbd, kt2) + tile_rows(rs) + ccol)
        vt = v_refs[pi][...]
        vt_list.append(vt.reshape(vt.shape[0] * vt.shape[1], vt.shape[2]).astype(BF16))
    carry_ref[...] = carry
    update(s_list, vt_list)

    @pl.when(g == pl.num_programs(1) - 1)
    def _():
        s = _dot(qbd, knt_ref[...]) + ccol - tile_rows(new_cumsum_t())
        kj = lax.broadcasted_iota(jnp.int32, (nrow, page), 1)
        qi = lax.broadcasted_iota(jnp.int32, (nrow, page), 0) // N_HEADS
        s = jnp.where(kj <= qi, s, NEG_INF)
        update([s], [vnt_ref[...]])
        o = acc_ref[...] / l_ref[...]
        for q_i in range(n_new):
            blk = o[q_i * N_HEADS:(q_i + 1) * N_HEADS, :]
            o_ref[q_i:q_i + 1, :] = jnp.sum(jnp.where(headmask, blk, 0.0), axis=0,
                                            keepdims=True).astype(o_ref.dtype)


def fox_sample_attention(qb, kb_new, vb_new, logf_new, cache_k, cache_v, cache_logf, page_table,
                         *, pages_per_step):
    nb, n_new, a = qb.shape
    n_pool, page, nh, hd = cache_k.shape
    n_pages = page_table.shape[1]
    pp = pages_per_step
    assert n_new <= page and nh == N_HEADS
    ckt = cache_k.transpose(0, 2, 3, 1)
    cvt = cache_v.transpose(0, 2, 3, 1)
    clt = cache_logf.transpose(0, 2, 1)
    lane_pad = ((0, 0), (0, 0), (0, page - n_new))
    knt = jnp.pad(kb_new.transpose(0, 2, 1), lane_pad)
    vnt = jnp.pad(vb_new.transpose(0, 2, 1), lane_pad)
    lfnt = jnp.pad(logf_new.transpose(0, 2, 1), lane_pad)
    pt = page_table.reshape(-1).astype(jnp.int32)

    def page_map(pi, ndim):
        return lambda b, g, pt_ref: (pt_ref[b * n_pages + (n_pages - 1 - (g * pp + pi))],) + (0,) * (ndim - 1)

    per_b = lambda b, g, pt_ref: (b, 0, 0)
    nrow = n_new * nh
    grid_spec = pltpu.PrefetchScalarGridSpec(
        num_scalar_prefetch=1,
        grid=(nb, n_pages // pp),
        in_specs=[pl.BlockSpec((None, n_new, a), per_b),
                  pl.BlockSpec((None, a, page), per_b),
                  pl.BlockSpec((None, a, page), per_b),
                  pl.BlockSpec((None, nh, page), per_b)]
                 + [pl.BlockSpec((None, nh, hd, page), page_map(pi, 4)) for pi in range(pp)]
                 + [pl.BlockSpec((None, nh, hd, page), page_map(pi, 4)) for pi in range(pp)]
                 + [pl.BlockSpec((None, nh, page), page_map(pi, 3)) for pi in range(pp)],
        out_specs=pl.BlockSpec((None, n_new, a), per_b),
        scratch_shapes=[pltpu.VMEM((nrow, 1), F32), pltpu.VMEM((nrow, 1), F32),
                        pltpu.VMEM((nrow, a), F32), pltpu.VMEM((nh, 1), F32),
                        pltpu.VMEM((nrow, a), BF16), pltpu.VMEM((nrow, 1), F32)],
    )
    return pl.pallas_call(
        functools.partial(_fox_sample_kernel, pages_per_step=pp, n_new=n_new),
        grid_spec=grid_spec,
        out_shape=jax.ShapeDtypeStruct((nb, n_new, a), BF16),
        compiler_params=_cparams("parallel", "arbitrary"),
        name="fox_sample",
    )(pt, qb, knt, vnt, lfnt, *([ckt] * pp), *([cvt] * pp), *([clt] * pp))


def fox_sample_attention_flat(qb, kb_new, vb_new, logf_new, cache_k, cache_v, cache_logf, page_table,
                              *, pages_per_step):
    nb, n_new, a = qb.shape
    n_pool, page, nh, hd = cache_k.shape
    n_pages = page_table.shape[1]
    pp = pages_per_step
    nrow = n_new * nh
    assert page * nh == N_HEADS * LANES and nrow <= LANES
    q32 = qb.reshape(nb, nrow, hd)
    pad = ((0, 0), (0, LANES - nrow), (0, 0))
    kn = jnp.pad(kb_new.reshape(nb, nrow, hd), pad)
    vn = jnp.pad(vb_new.reshape(nb, nrow, hd), pad)
    lfn = jnp.pad(logf_new.reshape(nb, 1, nrow), ((0, 0), (0, 0), (0, LANES - nrow)))
    lf_flat = cache_logf.reshape(n_pool, N_HEADS, LANES)
    pt = page_table.reshape(-1).astype(jnp.int32)

    def page_map(pi, ndim):
        return lambda b, g, pt_ref: (pt_ref[b * n_pages + (n_pages - 1 - (g * pp + pi))],) + (0,) * (ndim - 1)

    per_b = lambda b, g, pt_ref: (b, 0, 0)
    grid_spec = pltpu.PrefetchScalarGridSpec(
        num_scalar_prefetch=1,
        grid=(nb, n_pages // pp),
        in_specs=[pl.BlockSpec((None, nrow, hd), per_b),
                  pl.BlockSpec((None, LANES, hd), per_b),
                  pl.BlockSpec((None, LANES, hd), per_b),
                  pl.BlockSpec((None, 1, LANES), per_b)]
                 + [pl.BlockSpec((None, page, nh, hd), page_map(pi, 4)) for pi in range(pp)]
                 + [pl.BlockSpec((None, page, nh, hd), page_map(pi, 4)) for pi in range(pp)]
                 + [pl.BlockSpec((None, N_HEADS, LANES), page_map(pi, 3)) for pi in range(pp)],
        out_specs=pl.BlockSpec((None, nrow, hd), per_b),
        scratch_shapes=[pltpu.VMEM((nrow, 1), F32), pltpu.VMEM((nrow, 1), F32),
                        pltpu.VMEM((nrow, hd), F32), pltpu.VMEM((1, LANES), F32),
                        pltpu.VMEM((nrow, 1), F32)],
    )
    out = pl.pallas_call(
        functools.partial(_fox_sample_kernel, pages_per_step=pp),
        grid_spec=grid_spec,
        out_shape=jax.ShapeDtypeStruct((nb, nrow, hd), BF16),
        compiler_params=_cparams("parallel", "arbitrary"),
        name="fox_sample",
    )(pt, q32, kn, vn, lfn, *([cache_k] * pp), *([cache_v] * pp), *([lf_flat] * pp))
    return out.reshape(nb, n_new, a)


QROWS_PAD = 16


def _fox_sample_kernel_v1(pt_ref, q_ref, kn_ref, vn_ref, lfn_ref, usup_ref, *rest, pages_per_step, n_new):
    pp = pages_per_step
    k_refs = rest[:pp]
    v_refs = rest[pp:2 * pp]
    lf_refs = rest[2 * pp:3 * pp]
    o_ref, m_ref, l_ref, acc_ref, carry_ref, qbd_ref, ccol_ref = rest[3 * pp:]
    g = pl.program_id(1)
    nrow = n_new * N_HEADS
    width = q_ref.shape[1]

    sub_h = lax.broadcasted_iota(jnp.int32, (N_HEADS, width), 0)
    lane_h = lax.broadcasted_iota(jnp.int32, (N_HEADS, width), 1) // HEAD_DIM
    headmask = sub_h == lane_h
    e_sel = (lax.broadcasted_iota(jnp.int32, (nrow, N_HEADS), 0) % N_HEADS
             == lax.broadcasted_iota(jnp.int32, (nrow, N_HEADS), 1))
    e_f32 = e_sel.astype(F32)
    e_bf = e_sel.astype(BF16)

    def new_cumsum_rows():
        lfn = lfn_ref[...]
        rows, run = [], None
        for t in range(n_new):
            run = lfn[t:t + 1, :] if run is None else run + lfn[t:t + 1, :]
            rows.append(run)
        return rows

    @pl.when(g == 0)
    def _():
        m_ref[...] = jnp.full_like(m_ref, NEG_INF)
        l_ref[...] = jnp.zeros_like(l_ref)
        acc_ref[...] = jnp.zeros_like(acc_ref)
        carry_ref[...] = jnp.zeros_like(carry_ref)
        q = q_ref[...].astype(F32)
        cn = new_cumsum_rows()
        for qi in range(n_new):
            qrow = jnp.broadcast_to(q[qi:qi + 1, :], (N_HEADS, width))
            qbd_ref[qi * N_HEADS:(qi + 1) * N_HEADS, :] = jnp.where(headmask, qrow, 0.0).astype(BF16)
            ccol_ref[qi * N_HEADS:(qi + 1) * N_HEADS, :] = jnp.sum(
                e_f32[:N_HEADS, :] * cn[qi], axis=1, keepdims=True)

    def online_update(s, v_bf):
        m_old = m_ref[...]
        m_new = jnp.maximum(m_old, jnp.max(s, axis=1, keepdims=True))
        alpha = jnp.exp(m_old - m_new)
        p = jnp.exp(s - m_new)
        l_ref[...] = alpha * l_ref[...] + jnp.sum(p, axis=1, keepdims=True)
        m_ref[...] = m_new
        acc_ref[...] = alpha * acc_ref[...] + _dot(p.astype(BF16), v_bf)

    qbd = qbd_ref[...]
    ccol = ccol_ref[...]
    usup = usup_ref[...]
    for pi in range(pp):
        kb = k_refs[pi][...].astype(BF16)
        s = _dot_nt(qbd, kb)
        lf = lf_refs[pi][...]
        hi, mid, lo = _split3(lf)
        rs = _dot(usup, hi) + _dot(usup, mid) + _dot(usup, lo) + carry_ref[...]
        r_hi, r_mid, r_lo = _split3(rs)
        bias = _dot_nt(e_bf, r_hi) + _dot_nt(e_bf, r_mid) + _dot_nt(e_bf, r_lo)
        carry_ref[...] = carry_ref[...] + jnp.sum(lf, axis=0, keepdims=True)
        online_update(s + bias + ccol, v_refs[pi][...].astype(BF16))

    @pl.when(g == pl.num_programs(1) - 1)
    def _():
        cn = new_cumsum_rows()
        s = _dot_nt(qbd, kn_ref[...])
        kj = lax.broadcasted_iota(jnp.int32, (nrow, QROWS_PAD), 1)
        qi = lax.broadcasted_iota(jnp.int32, (nrow, QROWS_PAD), 0) // N_HEADS
        bias = jnp.zeros((nrow, QROWS_PAD), F32)
        for t in range(n_new):
            col = jnp.sum(e_f32 * cn[t], axis=1, keepdims=True)
            bias = jnp.where(kj == t, col, bias)
        s = s + ccol - bias
        s = jnp.where(kj <= qi, s, NEG_INF)
        online_update(s, vn_ref[...])
        o = acc_ref[...] / l_ref[...]
        for q_i in range(n_new):
            blk = o[q_i * N_HEADS:(q_i + 1) * N_HEADS, :]
            o_ref[q_i:q_i + 1, :] = jnp.sum(jnp.where(headmask, blk, 0.0), axis=0,
                                            keepdims=True).astype(o_ref.dtype)


def fox_sample_attention_v1(qb, kb_new, vb_new, logf_new, cache_k, cache_v, cache_logf, page_table,
                            *, pages_per_step):
    nb, n_new, a = qb.shape
    n_pool, page, _, _ = cache_k.shape
    n_pages = page_table.shape[1]
    pp = pages_per_step
    ck = cache_k.reshape(n_pool, page, a)
    cv = cache_v.reshape(n_pool, page, a)
    pad = ((0, 0), (0, QROWS_PAD - n_new), (0, 0))
    kn = jnp.pad(kb_new, pad)
    vn = jnp.pad(vb_new, pad)
    usup = jnp.triu(jnp.ones((page, page), F32), 1).astype(BF16)
    pt = page_table.reshape(-1).astype(jnp.int32)

    def page_map(pi):
        return lambda b, g, pt_ref: (pt_ref[b * n_pages + (n_pages - 1 - (g * pp + pi))], 0, 0)

    per_b = lambda b, g, pt_ref: (b, 0, 0)
    grid_spec = pltpu.PrefetchScalarGridSpec(
        num_scalar_prefetch=1,
        grid=(nb, n_pages // pp),
        in_specs=[pl.BlockSpec((None, n_new, a), per_b),
                  pl.BlockSpec((None, QROWS_PAD, a), per_b),
                  pl.BlockSpec((None, QROWS_PAD, a), per_b),
                  pl.BlockSpec((None, n_new, N_HEADS), per_b),
                  pl.BlockSpec((page, page), lambda b, g, pt_ref: (0, 0))]
                 + [pl.BlockSpec((None, page, a), page_map(pi)) for pi in range(pp)]
                 + [pl.BlockSpec((None, page, a), page_map(pi)) for pi in range(pp)]
                 + [pl.BlockSpec((None, page, N_HEADS), page_map(pi)) for pi in range(pp)],
        out_specs=pl.BlockSpec((None, n_new, a), per_b),
        scratch_shapes=[pltpu.VMEM((n_new * N_HEADS, 1), F32), pltpu.VMEM((n_new * N_HEADS, 1), F32),
                        pltpu.VMEM((n_new * N_HEADS, a), F32), pltpu.VMEM((1, N_HEADS), F32),
                        pltpu.VMEM((n_new * N_HEADS, a), BF16), pltpu.VMEM((n_new * N_HEADS, 1), F32)],
    )
    return pl.pallas_call(
        functools.partial(_fox_sample_kernel, pages_per_step=pp, n_new=n_new),
        grid_spec=grid_spec,
        out_shape=jax.ShapeDtypeStruct((nb, n_new, a), BF16),
        compiler_params=_cparams("parallel", "arbitrary"),
        name="fox_sample",
    )(pt, qb, kn, vn, logf_new, usup, *([ck] * pp), *([cv] * pp), *([cache_logf] * pp))


def _layer_norm(x, g, b):
    mu = jnp.mean(x, axis=-1, keepdims=True)
    xc = x - mu
    var = jnp.mean(xc * xc, axis=-1, keepdims=True)
    return xc * lax.rsqrt(var + LN_EPS) * g + b


SUBLANES = 8


def _to_row_tiles(x, ref):
    rows = x.shape[0]
    for c in range(SUBLANES):
        ref[pl.ds(c, rows, stride=SUBLANES), :] = x[:, c * LANES:(c + 1) * LANES]


def _from_row_tiles(ref, first_row, rows):
    return jnp.concatenate(
        [ref[pl.ds(first_row * SUBLANES + c, rows, stride=SUBLANES), :] for c in range(SUBLANES)], axis=1)


def _outproj_router_kernel(*refs, n_experts, aliased, n_tiles):
    if aliased:
        refs = refs[2:]
    ins, (h_ref, ht_ref, idx_ref, w_ref, rank_ref, cnt_out_ref, cnt_scr) = refs[:11], refs[11:]
    cnt_in_ref = ins[10]
    i = pl.program_id(0)

    @pl.when(i == 0)
    def _():
        cnt_scr[...] = cnt_in_ref[...]

    @pl.when(i < n_tiles)
    def _():
        _outproj_router_tile(*ins[:10], h_ref, ht_ref, idx_ref, w_ref, rank_ref, cnt_scr, n_experts=n_experts)

    @pl.when(i >= n_tiles)
    def _():
        h_ref[...] = jnp.zeros_like(h_ref)
        ht_ref[...] = jnp.zeros_like(ht_ref)

    cnt_out_ref[...] = cnt_scr[...]


def _outproj_router_tile(att_ref, ssm_ref, x_ref, woa_ref, wos_ref, g_ref, b_ref, rw_ref, rb_ref,
                         tri_ref, h_ref, ht_ref, idx_ref, w_ref, rank_ref, cnt_scr, *, n_experts):
    tm = x_ref.shape[0]
    att = att_ref[...]
    ssm = ssm_ref[...].reshape(tm, ssm_ref.shape[-1])
    mixed = _dot(att, woa_ref[...]) + _dot(ssm, wos_ref[...])
    h = _layer_norm(DN_ALPHA * x_ref[...] + mixed, g_ref[...], b_ref[...])
    h_ref[...] = h
    _to_row_tiles(h, ht_ref)
    h_hi = h.astype(BF16)
    h_lo = (h - h_hi.astype(F32)).astype(BF16)
    logits = _dot(h_hi, rw_ref[0]) + _dot(h_lo, rw_ref[0]) + _dot(h_hi, rw_ref[1]) + rb_ref[...]
    lane = lax.broadcasted_iota(jnp.int32, logits.shape, 1)
    logits = jnp.where(lane < n_experts, logits, NEG_INF)
    vals, idxs = [], []
    for _ in range(TOP_K):
        mx = jnp.max(logits, axis=1, keepdims=True)
        ix = jnp.min(jnp.where(logits == mx, lane, LANES), axis=1, keepdims=True)
        vals.append(mx)
        idxs.append(ix)
        logits = jnp.where(lane == ix, NEG_INF, logits)
    es = [jnp.exp(v - vals[0]) for v in vals]
    denom = es[0] + es[1] + es[2] + es[3]
    lane8 = lax.broadcasted_iota(jnp.int32, (tm, idx_ref.shape[1]), 1)
    idx_out = jnp.zeros(lane8.shape, jnp.int32)
    w_out = jnp.zeros(lane8.shape, F32)
    onehot = jnp.zeros(logits.shape, F32)
    for k in range(TOP_K):
        onehot = onehot + (lane == idxs[k]).astype(F32)
    before = _dot(tri_ref[...], onehot.astype(BF16)) + cnt_scr[...]
    cnt_scr[...] = cnt_scr[...] + jnp.sum(onehot, axis=0, keepdims=True)
    rank_out = jnp.zeros(lane8.shape, jnp.int32)
    for k in range(TOP_K):
        idx_out = jnp.where(lane8 == k, idxs[k], idx_out)
        w_out = jnp.where(lane8 == k, es[k] / denom, w_out)
        rank_k = jnp.sum(jnp.where(lane == idxs[k], before, 0.0), axis=1, keepdims=True)
        rank_out = jnp.where(lane8 == k, rank_k.astype(jnp.int32), rank_out)
    idx_ref[...] = idx_out
    w_ref[...] = w_out
    rank_ref[...] = rank_out


ROUTE_COLS = 8


def outproj_router(att, ssm, x2, w_out, ln_g, ln_b, router_w, router_b, counts_in, *, tm, m_total,
                   row_block_offset, h_all, ssm_chunked):
    m, d = x2.shape
    a = att.shape[1]
    n_experts = router_w.shape[1]
    nt = m // tm
    woa = w_out[:a].astype(BF16)
    wos = w_out[a:].astype(BF16)
    rw = jnp.pad(router_w.astype(F32), ((0, 0), (0, LANES - n_experts)))
    rw_hi = rw.astype(BF16)
    rw_parts = jnp.stack([rw_hi, (rw - rw_hi.astype(F32)).astype(BF16)])
    rb = jnp.pad(router_b.astype(F32), (0, LANES - n_experts)).reshape(1, LANES)
    aliased = h_all is not None
    n_fill = 0 if aliased else m_total // tm - nt - row_block_offset
    clamp = lambda i: jnp.minimum(i, nt - 1)
    row = lambda i: (clamp(i), 0)
    full = lambda i: (0, 0)
    if ssm_chunked is None:
        ssm_spec = pl.BlockSpec((tm, ssm.shape[-1]), row)
    else:
        nseq, slen, L = ssm_chunked
        tps = slen // tm
        ssm_spec = pl.BlockSpec((tm // L, None, L, ssm.shape[-1]),
                                lambda i: (clamp(i) % tps, clamp(i) // tps, 0, 0))
    in_specs = [pl.BlockSpec((tm, a), row), ssm_spec, pl.BlockSpec((tm, d), row),
                pl.BlockSpec((a, d), full), pl.BlockSpec((w_out.shape[0] - a, d), full),
                pl.BlockSpec((1, d), full), pl.BlockSpec((1, d), full),
                pl.BlockSpec((2, d, LANES), lambda i: (0, 0, 0)), pl.BlockSpec((1, LANES), full),
                pl.BlockSpec((tm, tm), full), pl.BlockSpec((1, LANES), full)]
    tri_strict = jnp.tril(jnp.ones((tm, tm), F32), -1).astype(BF16)
    args = [att, ssm, x2, woa, wos, ln_g.astype(F32).reshape(1, d), ln_b.astype(F32).reshape(1, d), rw_parts, rb,
            tri_strict, counts_in]
    assert d == SUBLANES * LANES
    aliases = {}
    if aliased:
        in_specs = [pl.BlockSpec(memory_space=pl.ANY)] * 2 + in_specs
        args = list(h_all) + args
        aliases = {0: 0, 1: 1}
    return pl.pallas_call(
        functools.partial(_outproj_router_kernel, n_experts=n_experts, aliased=aliased, n_tiles=nt),
        grid=(nt + n_fill,),
        in_specs=in_specs,
        out_specs=[pl.BlockSpec((tm, d), lambda i: (i + row_block_offset, 0)),
                   pl.BlockSpec((tm * SUBLANES, LANES), lambda i: (i + row_block_offset, 0)),
                   pl.BlockSpec((tm, ROUTE_COLS), row), pl.BlockSpec((tm, ROUTE_COLS), row),
                   pl.BlockSpec((tm, ROUTE_COLS), row), pl.BlockSpec((1, LANES), full)],
        out_shape=[jax.ShapeDtypeStruct((m_total, d), F32),
                   jax.ShapeDtypeStruct((m_total * SUBLANES, LANES), F32),
                   jax.ShapeDtypeStruct((m, ROUTE_COLS), jnp.int32),
                   jax.ShapeDtypeStruct((m, ROUTE_COLS), F32),
                   jax.ShapeDtypeStruct((m, ROUTE_COLS), jnp.int32),
                   jax.ShapeDtypeStruct((1, LANES), F32)],
        input_output_aliases=aliases,
        scratch_shapes=[pltpu.VMEM((1, LANES), F32)],
        compiler_params=_cparams("arbitrary"),
        name="outproj_router",
    )(*args)


DMA_UNROLL = 8


def _for_each_row(n_rows, fn):
    def body(j, carry):
        for u in range(DMA_UNROLL):
            fn(j * DMA_UNROLL + u, u)
        return carry
    lax.fori_loop(0, n_rows // DMA_UNROLL, body, 0)


def _tile_rows(ref, row_tile):
    return ref.at[pl.ds(pl.multiple_of(row_tile * SUBLANES, SUBLANES), SUBLANES), :]


def _scatter_rows_kernel(base_ref, ztile_ref, idx_ref, rank_ref, ht_ref, zeros_ref, xs_ref, sem, zsem, *, tile):
    n_assign = idx_ref.shape[1]
    n_experts = base_ref.shape[0]

    @pl.when(pl.program_id(0) == 0)
    def _():
        def zero_copy(e):
            row0 = pl.multiple_of(ztile_ref[e] * SUBLANES, tile * SUBLANES)
            return pltpu.make_async_copy(zeros_ref, xs_ref.at[pl.ds(row0, tile * SUBLANES), :], zsem)
        for e in range(n_experts):
            zero_copy(e).start()
        for e in range(n_experts):
            zero_copy(e).wait()

    def row_copy(a, dst_row):
        tok = lax.shift_right_logical(a, 2)
        return pltpu.make_async_copy(_tile_rows(ht_ref, tok), _tile_rows(xs_ref, dst_row), sem)

    def start(a, u):
        row_copy(a, base_ref[idx_ref[0, a]] + rank_ref[0, a]).start(priority=u % 2)

    def wait(a, u):
        row_copy(a, 0).wait()

    _for_each_row(n_assign, start)
    _for_each_row(n_assign, wait)


def scatter_rows(h_tiles, idx_flat, rank_flat, base, ztile, *, n_rows_out, tokens_per_tile, expert_tile):
    assert TOP_K == 4
    m = h_tiles.shape[0] // SUBLANES
    nt = m // tokens_per_tile
    na = tokens_per_tile * TOP_K
    grid_spec = pltpu.PrefetchScalarGridSpec(
        num_scalar_prefetch=2,
        grid=(nt,),
        in_specs=[pl.BlockSpec((None, 1, na), lambda i, b, z: (i, 0, 0), memory_space=pltpu.SMEM),
                  pl.BlockSpec((None, 1, na), lambda i, b, z: (i, 0, 0), memory_space=pltpu.SMEM),
                  pl.BlockSpec((tokens_per_tile * SUBLANES, LANES), lambda i, b, z: (i, 0)),
                  pl.BlockSpec((expert_tile * SUBLANES, LANES), lambda i, b, z: (0, 0))],
        out_specs=pl.BlockSpec(memory_space=pl.ANY),
        scratch_shapes=[pltpu.SemaphoreType.DMA, pltpu.SemaphoreType.DMA],
    )
    return pl.pallas_call(
        functools.partial(_scatter_rows_kernel, tile=expert_tile),
        grid_spec=grid_spec,
        out_shape=jax.ShapeDtypeStruct((n_rows_out * SUBLANES, LANES), h_tiles.dtype),
        compiler_params=_cparams("arbitrary"),
        name="moe_scatter",
    )(base, ztile, idx_flat.reshape(nt, 1, na), rank_flat.reshape(nt, 1, na), h_tiles,
      jnp.zeros((expert_tile * SUBLANES, LANES), h_tiles.dtype))


def _expert_kernel(te_ref, tv_ref, x_ref, wup_ref, bup_ref, wdn_ref, bdn_ref, y_ref):
    i = pl.program_id(0)
    de = wdn_ref.shape[0]
    tm = x_ref.shape[0] // SUBLANES

    @pl.when(tv_ref[i] > 0)
    def _():
        xb = _from_row_tiles(x_ref, 0, tm).astype(BF16)
        hcat = _dot(xb, wup_ref[...]) + bup_ref[...]
        x_glu = jnp.minimum(hcat[:, :de], SWIGLU_LIMIT)
        x_lin = jnp.clip(hcat[:, de:], -SWIGLU_LIMIT, SWIGLU_LIMIT)
        act = x_glu * jax.nn.sigmoid(SWIGLU_ALPHA * x_glu) * (x_lin + 1.0)
        _to_row_tiles(_dot(act.astype(BF16), wdn_ref[...]) + bdn_ref[...], y_ref)

    @pl.when(tv_ref[i] == 0)
    def _():
        y_ref[...] = jnp.zeros_like(y_ref)


def expert_mlp(xs, tile_expert, tile_valid, w_up, b_up, w_down, b_down, *, tm):
    npad = xs.shape[0] // SUBLANES
    n_e, d, d2 = w_up.shape
    de = w_down.shape[1]
    dout = w_down.shape[2]
    assert d == SUBLANES * LANES and dout == SUBLANES * LANES
    nt = npad // tm
    grid_spec = pltpu.PrefetchScalarGridSpec(
        num_scalar_prefetch=2,
        grid=(nt,),
        in_specs=[pl.BlockSpec((tm * SUBLANES, LANES), lambda i, te, tv: (jnp.where(tv[i] > 0, i, te[nt]), 0)),
                  pl.BlockSpec((None, d, d2), lambda i, te, tv: (te[i], 0, 0)),
                  pl.BlockSpec((None, 1, d2), lambda i, te, tv: (te[i], 0, 0)),
                  pl.BlockSpec((None, de, dout), lambda i, te, tv: (te[i], 0, 0)),
                  pl.BlockSpec((None, 1, dout), lambda i, te, tv: (te[i], 0, 0))],
        out_specs=pl.BlockSpec((tm * SUBLANES, LANES), lambda i, te, tv: (i, 0)),
    )
    return pl.pallas_call(
        _expert_kernel,
        grid_spec=grid_spec,
        out_shape=jax.ShapeDtypeStruct((npad * SUBLANES, LANES), F32),
        compiler_params=_cparams("arbitrary"),
        name="moe_experts",
    )(tile_expert, tile_valid, xs, w_up.astype(BF16),
      b_up.astype(F32).reshape(n_e, 1, d2), w_down.astype(BF16), b_down.astype(F32).reshape(n_e, 1, dout))


def _combine_ln_kernel(base_ref, idx_ref, rank_ref, h_ref, w_ref, g_ref, b_ref, ys_ref, o_ref, buf, sem):
    tm = h_ref.shape[0]
    n_assign = idx_ref.shape[1]

    def row_copy(a, src_row):
        dst = lax.bitwise_and(a, TOP_K - 1) * tm + lax.shift_right_logical(a, 2)
        return pltpu.make_async_copy(_tile_rows(ys_ref, src_row), _tile_rows(buf, dst), sem)

    def start(a, u):
        row_copy(a, base_ref[idx_ref[0, a]] + rank_ref[0, a]).start(priority=u % 2)

    def wait(a, u):
        row_copy(a, 0).wait()

    _for_each_row(n_assign, start)
    _for_each_row(n_assign, wait)
    w = w_ref[...]
    moe = w[:, 0:1] * _from_row_tiles(buf, 0, tm)
    for k in range(1, TOP_K):
        moe = moe + w[:, k:k + 1] * _from_row_tiles(buf, k * tm, tm)
    o_ref[...] = _layer_norm(DN_ALPHA * h_ref[...] + moe, g_ref[...], b_ref[...])


def combine_ln(h_all, ys, idx_flat, rank_flat, top_w, base, ln_g, ln_b, *, tm, m, row_block_offset):
    assert TOP_K == 4
    d = h_all.shape[1]
    nt = m // tm
    na = tm * TOP_K
    grid_spec = pltpu.PrefetchScalarGridSpec(
        num_scalar_prefetch=1,
        grid=(nt,),
        in_specs=[pl.BlockSpec((None, 1, na), lambda i, b: (i, 0, 0), memory_space=pltpu.SMEM),
                  pl.BlockSpec((None, 1, na), lambda i, b: (i, 0, 0), memory_space=pltpu.SMEM),
                  pl.BlockSpec((tm, d), lambda i, b: (i + row_block_offset, 0)),
                  pl.BlockSpec((tm, ROUTE_COLS), lambda i, b: (i, 0)),
                  pl.BlockSpec((1, d), lambda i, b: (0, 0)),
                  pl.BlockSpec((1, d), lambda i, b: (0, 0)),
                  pl.BlockSpec(memory_space=pl.ANY)],
        out_specs=pl.BlockSpec((tm, d), lambda i, b: (i, 0)),
        scratch_shapes=[pltpu.VMEM((TOP_K * tm * SUBLANES, LANES), F32), pltpu.SemaphoreType.DMA],
    )
    return pl.pallas_call(
        _combine_ln_kernel,
        grid_spec=grid_spec,
        out_shape=jax.ShapeDtypeStruct((m, d), F32),
        compiler_params=_cparams("arbitrary"),
        name="moe_combine_ln",
    )(base, idx_flat.reshape(nt, 1, na), rank_flat.reshape(nt, 1, na), h_all, top_w,
      ln_g.astype(F32).reshape(1, d), ln_b.astype(F32).reshape(1, d), ys)


def _tile_plan(counts, n_assign, tm):
    n_experts = counts.shape[0]
    n_tiles = n_assign // tm + n_experts
    ntile_e = (counts + tm - 1) // tm
    tile_end = jnp.cumsum(ntile_e)
    tile_start = tile_end - ntile_e
    tile_ids = jnp.arange(n_tiles, dtype=jnp.int32)
    tile_expert = jnp.minimum((tile_ids[:, None] >= tile_end[None, :]).sum(axis=1), n_experts - 1)
    n_used = tile_end[-1]
    tile_valid = (tile_ids < n_used).astype(jnp.int32)
    tile_expert = jnp.concatenate([tile_expert, jnp.maximum(n_used - 1, 0)[None]]).astype(jnp.int32)
    base = (tile_start * tm).astype(jnp.int32)
    ztile = jnp.where(ntile_e > 0, (tile_end - 1) * tm, (n_tiles - 1) * tm).astype(jnp.int32)
    return n_tiles, tile_expert, tile_valid, base, ztile


S5_CHUNK = 16
TOKEN_TILE = 512
ATTN_TILE = 1024
S5_CHUNKS_PER_TILE = 16
PAGES_PER_STEP = 32
EXPERT_TILE = 512
SCATTER_TILE = 512
COMBINE_TILE = 256


def kernel(x_prompt, x_sample, cache_k, cache_v, cache_logf, state_ssm_re, state_ssm_im, page_table, w_in, b_forget, ssm_a_re, ssm_a_im, ssm_log_dt, ssm_b_re, ssm_b_im, ssm_c_re, ssm_c_im, ssm_d, ssm_glu_w, ssm_glu_b, w_out, ln1_g, ln1_b, router_w, router_b, w_up, b_up, w_down, b_down, ln2_g, ln2_b):
    bsz, slen, d = x_prompt.shape
    dec_b, dec_t, _ = x_sample.shape
    m_p = bsz * slen
    m_s = dec_b * dec_t
    m_all = m_p + m_s
    n_groups, n_state = ssm_a_re.shape
    n_experts = router_w.shape[1]
    tm = TOKEN_TILE
    assert m_s == tm and slen % tm == 0 and m_p % tm == 0
    assert (m_all * TOP_K) % EXPERT_TILE == 0 and m_p % COMBINE_TILE == 0 and m_s % COMBINE_TILE == 0

    qb, k_p, v_p, kb, vb, logf_p, u_p = in_projection(x_prompt, w_in, b_forget, tm=tm, chunk_len=S5_CHUNK,
                                                      aug=True)
    att_p = fox_prompt_attention(qb, kb, vb, bsz, slen, tq=ATTN_TILE)
    tables_p = _s5_tables(ssm_a_re, ssm_a_im, ssm_log_dt, ssm_b_re, ssm_b_im, ssm_c_re, ssm_c_im, S5_CHUNK)
    zeros_state = jnp.zeros((bsz, n_groups, n_state), F32)
    ssm_p, hre_p, him_p = s5_mixer(u_p.reshape(m_p, u_p.shape[-1]), zeros_state, zeros_state, tables_p,
                                   ssm_d, ssm_glu_w, ssm_glu_b, nseq=bsz, chunk_len=S5_CHUNK,
                                   chunks_per_tile=S5_CHUNKS_PER_TILE, precise_state=False)

    qs, k_s, v_s, ksb, vsb, logf_s, u_s = in_projection(x_sample.reshape(1, m_s, d), w_in, b_forget,
                                                         tm=tm, chunk_len=None, aug=False)
    a = ATTN_WIDTH
    att_s = fox_sample_attention(qs.reshape(dec_b, dec_t, a), ksb.reshape(dec_b, dec_t, a),
                                 vsb.reshape(dec_b, dec_t, a), logf_s.reshape(dec_b, dec_t, N_HEADS),
                                 cache_k, cache_v, cache_logf, page_table, pages_per_step=PAGES_PER_STEP)
    tables_s = _s5_tables(ssm_a_re, ssm_a_im, ssm_log_dt, ssm_b_re, ssm_b_im, ssm_c_re, ssm_c_im, dec_t)
    ssm_s, hre_s, him_s = s5_mixer(u_s, state_ssm_re, state_ssm_im, tables_s, ssm_d, ssm_glu_w, ssm_glu_b,
                                   nseq=dec_b, chunk_len=dec_t, chunks_per_tile=1, precise_state=True)

    ssm_p4 = ssm_p.reshape(slen // S5_CHUNK, bsz, S5_CHUNK, ssm_p.shape[-1])
    zero_counts = jnp.zeros((1, LANES), F32)
    h_all, h_tiles, idx_p, w_p, rank_p, cnt_p = outproj_router(
        att_p, ssm_p4, x_prompt.reshape(m_p, d), w_out, ln1_g, ln1_b, router_w, router_b, zero_counts,
        tm=tm, m_total=m_all, row_block_offset=0, h_all=None, ssm_chunked=(bsz, slen, S5_CHUNK))
    h_all, h_tiles, idx_s, w_s, rank_s, cnt_all = outproj_router(
        att_s.reshape(m_s, a), ssm_s, x_sample.reshape(m_s, d), w_out, ln1_g, ln1_b, router_w, router_b, cnt_p,
        tm=tm, m_total=m_all, row_block_offset=m_p // tm, h_all=(h_all, h_tiles), ssm_chunked=None)

    counts = cnt_all[0, :n_experts].astype(jnp.int32)
    n_tiles, tile_expert, tile_valid, base, ztile = _tile_plan(counts, m_all * TOP_K, EXPERT_TILE)
    flat = lambda t: t[:, :TOP_K].reshape(-1)
    idx_pf, idx_sf, rank_pf, rank_sf = flat(idx_p), flat(idx_s), flat(rank_p), flat(rank_s)
    xs = scatter_rows(h_tiles, jnp.concatenate([idx_pf, idx_sf]), jnp.concatenate([rank_pf, rank_sf]), base, ztile,
                      n_rows_out=n_tiles * EXPERT_TILE, tokens_per_tile=SCATTER_TILE, expert_tile=EXPERT_TILE)
    ys = expert_mlp(xs, tile_expert, tile_valid, w_up, b_up, w_down, b_down, tm=EXPERT_TILE)
    y_p = combine_ln(h_all, ys, idx_pf, rank_pf, w_p, base, ln2_g, ln2_b, tm=COMBINE_TILE, m=m_p,
                     row_block_offset=0)
    y_s = combine_ln(h_all, ys, idx_sf, rank_sf, w_s, base, ln2_g, ln2_b, tm=COMBINE_TILE, m=m_s,
                     row_block_offset=m_p // COMBINE_TILE)

    hd = (N_HEADS, HEAD_DIM)
    return (y_p.reshape(bsz, slen, d), y_s.reshape(dec_b, dec_t, d),
            k_p.reshape(bsz, slen, *hd), v_p.reshape(bsz, slen, *hd),
            logf_p.reshape(bsz, slen, N_HEADS).astype(cache_logf.dtype),
            hre_p.astype(state_ssm_re.dtype), him_p.astype(state_ssm_im.dtype),
            k_s.reshape(dec_b, dec_t, *hd), v_s.reshape(dec_b, dec_t, *hd),
            logf_s.reshape(dec_b, dec_t, N_HEADS).astype(cache_logf.dtype),
            hre_s.astype(state_ssm_re.dtype), him_s.astype(state_ssm_im.dtype))
```

```python
import functools
import math

import jax
import jax.numpy as jnp
from jax import lax
from jax.experimental import pallas as pl
from jax.experimental.pallas import tpu as pltpu

N_HEADS = 8
HEAD_DIM = 64
ATTN_WIDTH = N_HEADS * HEAD_DIM
SSM_GROUP = 16
SSM_STATE = 64
TOP_K = 4
SWIGLU_LIMIT = 7.0
SWIGLU_ALPHA = 1.702
LN_EPS = 1e-5
DEPTH = 1
DN_ALPHA = (2.0 * DEPTH) ** 0.25
ATTN_SCALE = HEAD_DIM ** -0.5

LANES = 128
GROUPS_PER_LANE_BLOCK = LANES // SSM_GROUP
STATE_LANES = GROUPS_PER_LANE_BLOCK * 2 * SSM_STATE
VMEM_LIMIT_BYTES = 56 * 1024 * 1024

BF16 = jnp.bfloat16
F32 = jnp.float32
NEG_INF = float("-inf")


def _cparams(*sem):
    return pltpu.CompilerParams(dimension_semantics=sem, vmem_limit_bytes=VMEM_LIMIT_BYTES)


def _split3(x):
    hi = x.astype(BF16)
    r1 = x - hi.astype(F32)
    mid = r1.astype(BF16)
    lo = (r1 - mid.astype(F32)).astype(BF16)
    return hi, mid, lo


def _dot(a, b):
    return jnp.dot(a, b, preferred_element_type=F32)


def _dot_nt(a, b):
    return lax.dot_general(a, b, (((1,), (1,)), ((), ())), preferred_element_type=F32)


LOG2E = math.log2(math.e)
AUG_C_LANE = HEAD_DIM
AUG_ONE_LANE = HEAD_DIM + 3


def _aug_tables():
    e = jnp.zeros((LANES, 2 * N_HEADS * LANES), F32)
    ones = jnp.zeros((1, 2 * N_HEADS * LANES), F32)
    koff = N_HEADS * LANES
    for h in range(N_HEADS):
        for p in range(3):
            src = p * N_HEADS + h
            e = e.at[src, h * LANES + AUG_C_LANE + p].set(1.0)
            e = e.at[src, koff + h * LANES + AUG_ONE_LANE + p].set(-1.0)
            ones = ones.at[0, h * LANES + AUG_ONE_LANE + p].set(1.0)
            ones = ones.at[0, koff + h * LANES + AUG_C_LANE + p].set(1.0)
    return e.astype(BF16), ones


def _spread_heads(x):
    low = lax.broadcasted_iota(jnp.int32, (1, LANES), 1) < HEAD_DIM
    blocks = []
    for j in range(x.shape[1] // LANES):
        pair = x[:, j * LANES:(j + 1) * LANES]
        blocks.append(jnp.where(low, pair, 0.0))
        blocks.append(jnp.where(low, pltpu.roll(pair, HEAD_DIM, 1), 0.0))
    return jnp.concatenate(blocks, axis=1)


def _inproj_kernel(x_ref, wqkv_ref, wf_ref, wu_ref, bf_ref, *rest, chunk_len, aug):
    if aug:
        tri_ref, e_ref, ones_ref, q_ref, k_ref, v_ref, kb_ref, vb_ref, logf_ref, u_ref, carry_ref = rest
    else:
        q_ref, k_ref, v_ref, kb_ref, vb_ref, logf_ref, u_ref = rest
    xb = x_ref[...].astype(BF16)
    qkv = _dot(xb, wqkv_ref[...])
    a = ATTN_WIDTH
    q = qkv[:, :a]
    k = qkv[:, a:2 * a]
    v = qkv[:, 2 * a:]
    if aug:
        k_ref[...] = k.T
        v_ref[...] = v.T
    else:
        k_ref[...] = k
        v_ref[...] = v
    vb_ref[...] = v.astype(BF16)
    f = _dot(xb, wf_ref[...]) + bf_ref[...]
    logf = jnp.minimum(f, 0.0) - jnp.log1p(jnp.exp(-jnp.abs(f)))
    logf_ref[...] = logf[:, :N_HEADS]

    if aug:
        @pl.when(pl.program_id(1) == 0)
        def _():
            carry_ref[...] = jnp.zeros_like(carry_ref)
        hi, mid, lo = _split3(logf)
        tri = tri_ref[...]
        c = _dot(tri, hi) + _dot(tri, mid) + _dot(tri, lo) + carry_ref[...]
        carry_ref[...] = c[-1:, :]
        c_hi, c_mid, c_lo = _split3(c * LOG2E)
        lane = lax.broadcasted_iota(jnp.int32, (1, LANES), 1)
        pieces = jnp.where(lane < N_HEADS, c_hi.astype(F32),
                           jnp.where(lane < 2 * N_HEADS, pltpu.roll(c_mid.astype(F32), N_HEADS, 1),
                                     jnp.where(lane < 3 * N_HEADS, pltpu.roll(c_lo.astype(F32), 2 * N_HEADS, 1),
                                               0.0)))
        extra = _dot(pieces.astype(BF16), e_ref[...]) + ones_ref[...]
        half = N_HEADS * LANES
        q_ref[...] = (_spread_heads(q) + extra[:, :half]).astype(BF16)
        kb_ref[...] = (_spread_heads(k) + extra[:, half:]).astype(BF16)
    else:
        q_ref[...] = q.astype(BF16)
        kb_ref[...] = k.astype(BF16)

    u = _dot(xb, wu_ref[...])
    if chunk_len is None:
        u_ref[...] = u
    else:
        u_ref[...] = u.reshape(u.shape[0] // chunk_len, chunk_len, u.shape[1])


def in_projection(x, w_in, b_forget, *, tm, chunk_len, aug):
    nseq, slen, d = x.shape
    m = nseq * slen
    nt = slen // tm
    a = ATTN_WIDTH
    x2 = x.reshape(m, d)
    q_scale = ATTN_SCALE * LOG2E if aug else ATTN_SCALE
    col_scale = jnp.concatenate([jnp.full((a,), q_scale, F32), jnp.ones((2 * a,), F32)])
    wqkv = (w_in[:, :3 * a] * col_scale).astype(BF16)
    wf = jnp.pad(w_in[:, 3 * a:3 * a + N_HEADS], ((0, 0), (0, LANES - N_HEADS))).astype(BF16)
    wu = w_in[:, 3 * a + N_HEADS:].astype(BF16)
    ssm_w = wu.shape[1]
    bf = jnp.pad(b_forget.astype(F32), (0, LANES - N_HEADS)).reshape(1, LANES)

    row = lambda b, i: (b * nt + i, 0)
    full = lambda b, i: (0, 0)
    tok = lambda w, dt: jax.ShapeDtypeStruct((m, w), dt)
    if chunk_len is None:
        u_shape = tok(ssm_w, F32)
        u_spec = pl.BlockSpec((tm, ssm_w), row)
    else:
        u_shape = jax.ShapeDtypeStruct((slen // chunk_len, nseq, chunk_len, ssm_w), F32)
        u_spec = pl.BlockSpec((tm // chunk_len, None, chunk_len, ssm_w), lambda b, i: (i, b, 0, 0))
    in_specs = [pl.BlockSpec((tm, d), row),
                pl.BlockSpec((d, 3 * a), full),
                pl.BlockSpec((d, LANES), full),
                pl.BlockSpec((d, ssm_w), full),
                pl.BlockSpec((1, LANES), full)]
    args = [x2, wqkv, wf, wu, bf]
    qk_w = a
    scratch = []
    kv_spec = pl.BlockSpec((tm, a), row)
    kv_shape = tok(a, F32)
    if aug:
        kv_spec = pl.BlockSpec((None, a, tm), lambda b, i: (b, 0, i))
        kv_shape = jax.ShapeDtypeStruct((nseq, a, slen), F32)
        e_mat, ones_row = _aug_tables()
        qk_w = N_HEADS * LANES
        in_specs += [pl.BlockSpec((tm, tm), full), pl.BlockSpec((LANES, 2 * qk_w), full),
                     pl.BlockSpec((1, 2 * qk_w), full)]
        args += [jnp.tril(jnp.ones((tm, tm), F32)).astype(BF16), e_mat, ones_row]
        scratch = [pltpu.VMEM((1, LANES), F32)]
    return pl.pallas_call(
        functools.partial(_inproj_kernel, chunk_len=chunk_len, aug=aug),
        grid=(nseq, nt),
        in_specs=in_specs,
        out_specs=[pl.BlockSpec((tm, qk_w), row), kv_spec, kv_spec,
                   pl.BlockSpec((tm, qk_w), row), pl.BlockSpec((tm, a), row),
                   pl.BlockSpec((tm, N_HEADS), row), u_spec],
        out_shape=[tok(qk_w, BF16), kv_shape, kv_shape, tok(qk_w, BF16), tok(a, BF16),
                   tok(N_HEADS, F32), u_shape],
        scratch_shapes=scratch,
        compiler_params=_cparams("parallel", "arbitrary"),
        name="in_projection",
    )(*args)


def _fox_prompt_kernel(q_ref, k_ref, v_ref, o_ref, m_ref, l_ref, acc_ref):
    i = pl.program_id(1)
    kk = pl.program_id(2)
    nk = pl.num_programs(2)
    tq = q_ref.shape[0]
    tk = k_ref.shape[0]

    @pl.when(kk == 0)
    def _():
        m_ref[...] = jnp.full_like(m_ref, NEG_INF)
        l_ref[...] = jnp.zeros_like(l_ref)
        acc_ref[...] = jnp.zeros_like(acc_ref)

    def step(masked):
        lane = lax.broadcasted_iota(jnp.int32, (1, LANES), 1)
        low = lane < HEAD_DIM
        if masked:
            qpos = lax.broadcasted_iota(jnp.int32, (tq, tk), 0)
            kpos = lax.broadcasted_iota(jnp.int32, (tq, tk), 1)
            causal = kpos <= qpos
        for blk in range(N_HEADS // 2):
            sl = slice(blk * LANES, (blk + 1) * LANES)
            v = v_ref[:, sl]
            pvs, alphas = [], []
            for half in range(2):
                h = 2 * blk + half
                hs = slice(h * LANES, (h + 1) * LANES)
                s = _dot_nt(q_ref[:, hs], k_ref[:, hs])
                if masked:
                    s = jnp.where(causal, s, NEG_INF)
                m_old = m_ref[:, h:h + 1]
                m_new = jnp.maximum(m_old, jnp.max(s, axis=1, keepdims=True))
                alpha = jnp.exp2(m_old - m_new)
                p = jnp.exp2(s - m_new)
                l_ref[:, h:h + 1] = alpha * l_ref[:, h:h + 1] + jnp.sum(p, axis=1, keepdims=True)
                m_ref[:, h:h + 1] = m_new
                pvs.append(_dot(p.astype(BF16), v))
                alphas.append(alpha)
            alpha2 = jnp.where(low, alphas[0], alphas[1])
            pv2 = jnp.where(low, pvs[0], pvs[1])
            acc_ref[:, sl] = alpha2 * acc_ref[:, sl] + pv2

    @pl.when(kk < i)
    def _():
        step(False)

    @pl.when(kk == i)
    def _():
        step(True)

    @pl.when(kk == nk - 1)
    def _():
        lane = lax.broadcasted_iota(jnp.int32, (1, LANES), 1)
        low = lane < HEAD_DIM
        for blk in range(N_HEADS // 2):
            sl = slice(blk * LANES, (blk + 1) * LANES)
            l2 = jnp.where(low, l_ref[:, 2 * blk:2 * blk + 1], l_ref[:, 2 * blk + 1:2 * blk + 2])
            o_ref[:, sl] = (acc_ref[:, sl] / l2).astype(o_ref.dtype)


def fox_prompt_attention(q_aug, k_aug, vb, nseq, slen, *, tq):
    m = nseq * slen
    nq = slen // tq
    a = ATTN_WIDTH
    aw = q_aug.shape[1]
    qmap = lambda b, i, kk: (b * nq + i, 0)
    kmap = lambda b, i, kk: (b * nq + jnp.minimum(kk, i), 0)
    return pl.pallas_call(
        _fox_prompt_kernel,
        grid=(nseq, nq, nq),
        in_specs=[pl.BlockSpec((tq, aw), qmap),
                  pl.BlockSpec((tq, aw), kmap),
                  pl.BlockSpec((tq, a), kmap)],
        out_specs=pl.BlockSpec((tq, a), qmap),
        out_shape=jax.ShapeDtypeStruct((m, a), BF16),
        scratch_shapes=[pltpu.VMEM((tq, N_HEADS), F32), pltpu.VMEM((tq, N_HEADS), F32),
                        pltpu.VMEM((tq, a), F32)],
        compiler_params=_cparams("parallel", "parallel", "arbitrary"),
        name="fox_prompt",
    )(q_aug, k_aug, vb)


def _s5_tables(a_re, a_im, log_dt, b_re, b_im, c_re, c_im, chunk_len):
    g, n = a_re.shape
    cdim = b_re.shape[2]
    nblk = g // GROUPS_PER_LANE_BLOCK
    gb = GROUPS_PER_LANE_BLOCK
    a_re = a_re.astype(F32)
    a_im = a_im.astype(F32)
    b_re = b_re.astype(F32)
    b_im = b_im.astype(F32)
    c_re = c_re.astype(F32)
    c_im = c_im.astype(F32)
    dt = jnp.exp(log_dt.astype(F32))[:, None]
    mag = jnp.exp(a_re * dt)
    ar = mag * jnp.cos(a_im * dt)
    ai = mag * jnp.sin(a_im * dt)
    den = a_re * a_re + a_im * a_im
    cr = ((ar - 1.0) * a_re + ai * a_im) / den
    ci = (ai * a_re - (ar - 1.0) * a_im) / den
    bbr = cr[..., None] * b_re - ci[..., None] * b_im
    bbi = cr[..., None] * b_im + ci[..., None] * b_re
    pr = [jnp.ones_like(ar)]
    pi = [jnp.zeros_like(ai)]
    for _ in range(chunk_len):
        pr_n = pr[-1] * ar - pi[-1] * ai
        pi_n = pr[-1] * ai + pi[-1] * ar
        pr.append(pr_n)
        pi.append(pi_n)
    pr = jnp.stack(pr)
    pi = jnp.stack(pi)
    L = chunk_len
    pbr = pr[:L, :, :, None] * bbr[None] - pi[:L, :, :, None] * bbi[None]
    pbi = pr[:L, :, :, None] * bbi[None] + pi[:L, :, :, None] * bbr[None]
    kmat = jnp.einsum('gon,lgni->lgoi', c_re, pbr) - jnp.einsum('gon,lgni->lgoi', c_im, pbi)
    eye = jnp.eye(gb, dtype=F32)
    bd = jnp.einsum('lqgoi,gh->lqgiho', kmat.reshape(L, nblk, gb, cdim, cdim), eye)
    bd = bd.reshape(L, nblk, gb * cdim, gb * cdim)
    steps = jnp.arange(L)
    shift = (steps[None, None, :] - steps[None, :, None] == steps[:, None, None]).astype(F32)
    tmat = jnp.einsum('lst,lqrc->qsrtc', shift, bd, precision=lax.Precision.HIGHEST)
    tmat = tmat.reshape(nblk, L * gb * cdim, L * gb * cdim)
    wr = pbr[::-1].reshape(L, nblk, gb, n, cdim)
    wi = pbi[::-1].reshape(L, nblk, gb, n, cdim)
    wri = jnp.stack([wr, wi], axis=3)
    w2 = wri.transpose(0, 1, 2, 5, 3, 4).reshape(L, nblk, gb, cdim, 2 * n)
    wmat = jnp.einsum('sqgix,gh->qsgihx', w2, eye).reshape(nblk, L * gb * cdim, gb * 2 * n)
    cpr = c_re[None] * pr[1:, :, None, :] - c_im[None] * pi[1:, :, None, :]
    cpi = c_re[None] * pi[1:, :, None, :] + c_im[None] * pr[1:, :, None, :]
    vri = jnp.stack([cpr, -cpi], axis=1).reshape(L, 2, nblk, gb, cdim, n)
    vs = vri.transpose(2, 3, 1, 5, 0, 4).reshape(nblk, gb, 2 * n, L, cdim)
    place = jnp.einsum('gh,op->goph', eye, jnp.eye(cdim, dtype=F32)).transpose(0, 1, 3, 2)
    place = place.reshape(gb, cdim, gb * cdim)
    vmat = jnp.einsum('qgxto,gop->qgxtp', vs, place, precision=lax.Precision.HIGHEST)
    vmat = vmat.reshape(nblk, gb * 2 * n, L * gb * cdim)
    al_r = pr[L].reshape(nblk, gb, n)
    al_i = pi[L].reshape(nblk, gb, n)
    a1 = jnp.stack([al_r, al_r], axis=2).reshape(nblk, 1, gb * 2 * n)
    a2 = jnp.stack([-al_i, al_i], axis=2).reshape(nblk, 1, gb * 2 * n)
    return tmat, wmat, vmat, a1, a2


def _gelu_exact(y):
    return 0.5 * y * (1.0 + lax.erf(y * (2.0 ** -0.5)))


def _s5_kernel(u_ref, h0_ref, t_ref, w_ref, v_ref, a1_ref, a2_ref, d_ref, gw_ref, gb_ref,
               y_ref, hout_ref, h_scr, hp_scr, y_scr, *, chunk_len, nseq, precise_state):
    i = pl.program_id(1)
    L = chunk_len
    rows = u_ref.shape[0]
    n = rows // L
    nj = n // nseq

    @pl.when(i == 0)
    def _():
        h_scr[...] = h0_ref[...]

    xs = [u_ref[pl.ds(t, n, stride=L), :] for t in range(L)] if L > 1 else [u_ref[...]]
    x32 = jnp.concatenate(xs, axis=1)
    xb = x32.astype(BF16)
    w = w_ref[0]
    s = _dot(xb, w)
    if precise_state:
        xlo = (x32 - xb.astype(F32)).astype(BF16)
        s = s + _dot(xlo, w) + _dot(xb, w_ref[1])
    a1 = a1_ref[...]
    a2 = a2_ref[...]
    h = h_scr[...]
    width = h.shape[1]
    is_re = (lax.broadcasted_iota(jnp.int32, (1, width), 1) % (2 * SSM_STATE)) < SSM_STATE
    for j in range(nj):
        hp_scr[j * nseq:(j + 1) * nseq, :] = h
        swapped = jnp.where(is_re, pltpu.roll(h, width - SSM_STATE, 1), pltpu.roll(h, SSM_STATE, 1))
        h = a1 * h + a2 * swapped + s[j * nseq:(j + 1) * nseq, :]
    h_scr[...] = h
    y = _dot(xb, t_ref[...]) + _dot(hp_scr[...].astype(BF16), v_ref[...])
    if L > 1:
        for t in range(L):
            y_scr[pl.ds(t, n, stride=L), :] = y[:, t * LANES:(t + 1) * LANES]
        ytok = y_scr[...]
    else:
        ytok = y
    ytok = ytok + d_ref[...] * u_ref[...]
    g = _gelu_exact(ytok)
    z = _dot(g.astype(BF16), gw_ref[...]) + gb_ref[...]
    y_ref[...] = (g * jax.nn.sigmoid(z)).astype(y_ref.dtype)

    @pl.when(i == pl.num_programs(1) - 1)
    def _():
        hout_ref[...] = h


def s5_mixer(u_rows, h0_re, h0_im, tables, ssm_d, glu_w, glu_b, *, nseq, chunk_len, chunks_per_tile,
             precise_state):
    tmat, wmat, vmat, a1, a2 = tables
    mrows, width = u_rows.shape
    nblk = width // LANES
    gb = GROUPS_PER_LANE_BLOCK
    L = chunk_len
    rows_tile = chunks_per_tile * nseq * L
    nt = mrows // rows_tile
    g_total = h0_re.shape[1]
    n = h0_re.shape[2]
    h0 = jnp.stack([h0_re.astype(F32), h0_im.astype(F32)], axis=2)
    h0 = h0.reshape(nseq, nblk, gb * 2 * n).transpose(1, 0, 2)
    drow = ssm_d.astype(F32).reshape(nblk, 1, LANES)
    gwb = jnp.einsum('qgce,gh->qgche', glu_w.astype(F32).reshape(nblk, gb, SSM_GROUP, SSM_GROUP),
                     jnp.eye(gb, dtype=F32)).reshape(nblk, LANES, LANES).astype(BF16)
    gbrow = glu_b.astype(F32).reshape(nblk, 1, LANES)
    kl = L * LANES
    sl = STATE_LANES
    blk = lambda q, i: (q, 0, 0)
    w_hi = wmat.astype(BF16)
    if precise_state:
        w_parts = jnp.stack([w_hi, (wmat - w_hi.astype(F32)).astype(BF16)], axis=1)
    else:
        w_parts = w_hi[:, None]
    nparts = w_parts.shape[1]
    y, hout = pl.pallas_call(
        functools.partial(_s5_kernel, chunk_len=L, nseq=nseq, precise_state=precise_state),
        grid=(nblk, nt),
        in_specs=[pl.BlockSpec((rows_tile, LANES), lambda q, i: (i, q)),
                  pl.BlockSpec((None, nseq, sl), blk),
                  pl.BlockSpec((None, kl, kl), blk),
                  pl.BlockSpec((None, nparts, kl, sl), lambda q, i: (q, 0, 0, 0)),
                  pl.BlockSpec((None, sl, kl), blk),
                  pl.BlockSpec((None, 1, sl), blk),
                  pl.BlockSpec((None, 1, sl), blk),
                  pl.BlockSpec((None, 1, LANES), blk),
                  pl.BlockSpec((None, LANES, LANES), blk),
                  pl.BlockSpec((None, 1, LANES), blk)],
        out_specs=[pl.BlockSpec((rows_tile, LANES), lambda q, i: (i, q)),
                   pl.BlockSpec((None, nseq, sl), blk)],
        out_shape=[jax.ShapeDtypeStruct((mrows, width), BF16),
                   jax.ShapeDtypeStruct((nblk, nseq, sl), F32)],
        scratch_shapes=[pltpu.VMEM((nseq, sl), F32),
                        pltpu.VMEM((chunks_per_tile * nseq, sl), F32),
                        pltpu.VMEM((rows_tile, LANES), F32)],
        compiler_params=_cparams("parallel", "arbitrary"),
        name="s5_mixer",
    )(u_rows, h0, tmat.astype(BF16), w_parts, vmat.astype(BF16), a1, a2, drow, gwb, gbrow)
    hout = hout.transpose(1, 0, 2).reshape(nseq, g_total, 2, n)
    return y, hout[:, :, 0, :], hout[:, :, 1, :]


def _flat_shift(x, s):
    rows, lanes = x.shape
    row = lax.broadcasted_iota(jnp.int32, x.shape, 0)
    if s < lanes:
        lane = lax.broadcasted_iota(jnp.int32, x.shape, 1)
        y = pltpu.roll(x, lanes - s, 1)
        ynext = pltpu.roll(y, rows - 1, 0)
        return jnp.where(lane >= lanes - s, jnp.where(row < rows - 1, ynext, 0.0), y)
    k = s // lanes
    return jnp.where(row < rows - k, pltpu.roll(x, rows - k, 0), 0.0)


def _suffix_sum_exclusive(x):
    e = _flat_shift(x, N_HEADS)
    s = N_HEADS
    while s < x.size:
        e = e + _flat_shift(e, s)
        s *= 2
    return e


def _head_totals(x):
    t = jnp.sum(x, axis=0, keepdims=True)
    s = N_HEADS
    while s < x.shape[1]:
        t = t + pltpu.roll(t, s, 1)
        s *= 2
    return t


def _fox_sample_kernel_flat(pt_ref, q_ref, kn_ref, vn_ref, lfn_ref, *rest, pages_per_step):
    pp = pages_per_step
    k_refs = rest[:pp]
    v_refs = rest[pp:2 * pp]
    lf_refs = rest[2 * pp:3 * pp]
    o_ref, m_ref, l_ref, acc_ref, carry_ref, ccol_ref = rest[3 * pp:]
    g = pl.program_id(1)
    nrow = q_ref.shape[0]
    row_h = lax.broadcasted_iota(jnp.int32, (nrow, LANES), 0) % N_HEADS
    lane = lax.broadcasted_iota(jnp.int32, (nrow, LANES), 1)
    same_head = row_h == lane % N_HEADS

    def new_cumsum():
        c = lfn_ref[...]
        s = N_HEADS
        while s < nrow:
            c = c + pltpu.roll(c, s, 1)
            s *= 2
        return c

    @pl.when(g == 0)
    def _():
        m_ref[...] = jnp.full_like(m_ref, NEG_INF)
        l_ref[...] = jnp.zeros_like(l_ref)
        acc_ref[...] = jnp.zeros_like(acc_ref)
        carry_ref[...] = jnp.zeros_like(carry_ref)
        own = lax.broadcasted_iota(jnp.int32, (nrow, LANES), 0) == lane
        ccol_ref[...] = jnp.sum(jnp.where(own, new_cumsum(), 0.0), axis=1, keepdims=True)

    def update(blocks, v_list):
        m_old = m_ref[...]
        mx = None
        for grp in blocks:
            for b in grp:
                mx = b if mx is None else jnp.maximum(mx, b)
        m_new = jnp.maximum(m_old, jnp.max(mx, axis=1, keepdims=True))
        alpha = jnp.exp(m_old - m_new)
        psum = jnp.zeros((nrow, LANES), F32)
        pv = jnp.zeros(acc_ref.shape, F32)
        for grp, v in zip(blocks, v_list):
            ps = [jnp.exp(b - m_new) for b in grp]
            for p in ps:
                psum = psum + p
            pcat = ps[0] if len(ps) == 1 else jnp.concatenate(ps, axis=1)
            pv = pv + _dot(pcat.astype(BF16), v)
        l_ref[...] = alpha * l_ref[...] + jnp.sum(psum, axis=1, keepdims=True)
        m_ref[...] = m_new
        acc_ref[...] = alpha * acc_ref[...] + pv

    q = q_ref[...]
    ccol = ccol_ref[...]
    carry = carry_ref[...]
    blocks, v_list = [], []
    for pi in range(pp):
        kpage = k_refs[pi][...]
        nkeys = kpage.shape[0] * kpage.shape[1]
        k2 = kpage.reshape(nkeys, kpage.shape[2]).astype(BF16)
        s = _dot_nt(q, k2)
        lf8 = lf_refs[pi][...]
        rs8 = _suffix_sum_exclusive(lf8) + carry
        carry = carry + _head_totals(lf8)
        grp = []
        for r in range(nkeys // LANES):
            sb = s[:, r * LANES:(r + 1) * LANES] + rs8[r:r + 1, :] + ccol
            grp.append(jnp.where(same_head, sb, NEG_INF))
        blocks.append(grp)
        v_list.append(v_refs[pi][...].reshape(nkeys, kpage.shape[2]).astype(BF16))
    carry_ref[...] = carry
    update(blocks, v_list)

    @pl.when(g == pl.num_programs(1) - 1)
    def _():
        s = _dot_nt(q, kn_ref[...])
        s = s + ccol - new_cumsum()
        causal = lane // N_HEADS <= lax.broadcasted_iota(jnp.int32, (nrow, LANES), 0) // N_HEADS
        s = jnp.where(same_head & causal & (lane < nrow), s, NEG_INF)
        update([[s]], [vn_ref[...]])
        o_ref[...] = (acc_ref[...] / l_ref[...]).astype(o_ref.dtype)


def _lane_shift_down(x, s):
    lanes = x.shape[1]
    lane = lax.broadcasted_iota(jnp.int32, x.shape, 1)
    return jnp.where(lane < lanes - s, pltpu.roll(x, lanes - s, 1), 0.0)


def _lane_shift_up(x, s):
    lane = lax.broadcasted_iota(jnp.int32, x.shape, 1)
    return jnp.where(lane >= s, pltpu.roll(x, s, 1), 0.0)


def _fox_sample_kernel(pt_ref, q_ref, knt_ref, vnt_ref, lfnt_ref, *rest, pages_per_step, n_new):
    pp = pages_per_step
    k_refs = rest[:pp]
    v_refs = rest[pp:2 * pp]
    lf_refs = rest[2 * pp:3 * pp]
    o_ref, m_ref, l_ref, acc_ref, carry_ref, qbd_ref, ccol_ref = rest[3 * pp:]
    g = pl.program_id(1)
    nrow = n_new * N_HEADS
    width = q_ref.shape[1]
    page = lf_refs[0].shape[1]
    headmask = (lax.broadcasted_iota(jnp.int32, (N_HEADS, width), 0)
                == lax.broadcasted_iota(jnp.int32, (N_HEADS, width), 1) // HEAD_DIM)
    lane = lax.broadcasted_iota(jnp.int32, (N_HEADS, page), 1)

    def new_cumsum_t():
        c = lfnt_ref[...]
        s = 1
        while s < n_new:
            c = c + _lane_shift_up(c, s)
            s *= 2
        return c

    def tile_rows(x):
        return jnp.concatenate([x] * n_new, axis=0)

    @pl.when(g == 0)
    def _():
        m_ref[...] = jnp.full_like(m_ref, NEG_INF)
        l_ref[...] = jnp.zeros_like(l_ref)
        acc_ref[...] = jnp.zeros_like(acc_ref)
        carry_ref[...] = jnp.zeros_like(carry_ref)
        q = q_ref[...].astype(F32)
        cn = new_cumsum_t()
        for qi in range(n_new):
            qrow = jnp.broadcast_to(q[qi:qi + 1, :], (N_HEADS, width))
            qbd_ref[qi * N_HEADS:(qi + 1) * N_HEADS, :] = jnp.where(headmask, qrow, 0.0).astype(BF16)
            ccol_ref[qi * N_HEADS:(qi + 1) * N_HEADS, :] = jnp.sum(
                jnp.where(lane == qi, cn, 0.0), axis=1, keepdims=True)

    def update(s_list, vt_list):
        m_old = m_ref[...]
        mx = s_list[0]
        for s in s_list[1:]:
            mx = jnp.maximum(mx, s)
        m_new = jnp.maximum(m_old, jnp.max(mx, axis=1, keepdims=True))
        alpha = jnp.exp(m_old - m_new)
        psum = jnp.zeros(s_list[0].shape, F32)
        pv = jnp.zeros(acc_ref.shape, F32)
        for s, vt in zip(s_list, vt_list):
            p = jnp.exp(s - m_new)
            psum = psum + p
            pv = pv + _dot_nt(p.astype(BF16), vt)
        l_ref[...] = alpha * l_ref[...] + jnp.sum(psum, axis=1, keepdims=True)
        m_ref[...] = m_new
        acc_ref[...] = alpha * acc_ref[...] + pv

    qbd = qbd_ref[...]
    ccol = ccol_ref[...]
    carry = carry_ref[...]
    s_list, vt_list = [], []
    for pi in range(pp):
        kt = k_refs[pi][...]
        kt2 = kt.reshape(kt.shape[0] * kt.shape[1], kt.shape[2]).astype(BF16)
        lf = lf_refs[pi][...]
        rs = _lane_shift_down(lf, 1)
        s = 1
        while s < page:
            rs = rs + _lane_shift_down(rs, s)
            s *= 2
        rs = rs + carry
        carry = carry + jnp.sum(lf, axis=1, keepdims=True)
        s_list.append(_dot(qbd, kt2) + tile_rows(rs) + ccol)
        vt = v_refs[pi][...]
        vt_list.append(vt.reshape(vt.shape[0] * vt.shape[1], vt.shape[2]).astype(BF16))
    carry_ref[...] = carry
    update(s_list, vt_list)

    @pl.when(g == pl.num_programs(1) - 1)
    def _():
        s = _dot(qbd, knt_ref[...]) + ccol - tile_rows(new_cumsum_t())
        kj = lax.broadcasted_iota(jnp.int32, (nrow, page), 1)
        qi = lax.broadcasted_iota(jnp.int32, (nrow, page), 0) // N_HEADS
        s = jnp.where(kj <= qi, s, NEG_INF)
        update([s], [vnt_ref[...]])
        o = acc_ref[...] / l_ref[...]
        for q_i in range(n_new):
            blk = o[q_i * N_HEADS:(q_i + 1) * N_HEADS, :]
            o_ref[q_i:q_i + 1, :] = jnp.sum(jnp.where(headmask, blk, 0.0), axis=0,
                                            keepdims=True).astype(o_ref.dtype)


def fox_sample_attention(qb, kb_new, vb_new, logf_new, cache_k, cache_v, cache_logf, page_table,
                         *, pages_per_step):
    nb, n_new, a = qb.shape
    n_pool, page, nh, hd = cache_k.shape
    n_pages = page_table.shape[1]
    pp = pages_per_step
    assert n_new <= page and nh == N_HEADS
    ckt = cache_k.transpose(0, 2, 3, 1)
    cvt = cache_v.transpose(0, 2, 3, 1)
    clt = cache_logf.transpose(0, 2, 1)
    lane_pad = ((0, 0), (0, 0), (0, page - n_new))
    knt = jnp.pad(kb_new.transpose(0, 2, 1), lane_pad)
    vnt = jnp.pad(vb_new.transpose(0, 2, 1), lane_pad)
    lfnt = jnp.pad(logf_new.transpose(0, 2, 1), lane_pad)
    pt = page_table.reshape(-1).astype(jnp.int32)

    def page_map(pi, ndim):
        return lambda b, g, pt_ref: (pt_ref[b * n_pages + (n_pages - 1 - (g * pp + pi))],) + (0,) * (ndim - 1)

    per_b = lambda b, g, pt_ref: (b, 0, 0)
    nrow = n_new * nh
    grid_spec = pltpu.PrefetchScalarGridSpec(
        num_scalar_prefetch=1,
        grid=(nb, n_pages // pp),
        in_specs=[pl.BlockSpec((None, n_new, a), per_b),
                  pl.BlockSpec((None, a, page), per_b),
                  pl.BlockSpec((None, a, page), per_b),
                  pl.BlockSpec((None, nh, page), per_b)]
                 + [pl.BlockSpec((None, nh, hd, page), page_map(pi, 4)) for pi in range(pp)]
                 + [pl.BlockSpec((None, nh, hd, page), page_map(pi, 4)) for pi in range(pp)]
                 + [pl.BlockSpec((None, nh, page), page_map(pi, 3)) for pi in range(pp)],
        out_specs=pl.BlockSpec((None, n_new, a), per_b),
        scratch_shapes=[pltpu.VMEM((nrow, 1), F32), pltpu.VMEM((nrow, 1), F32),
                        pltpu.VMEM((nrow, a), F32), pltpu.VMEM((nh, 1), F32),
                        pltpu.VMEM((nrow, a), BF16), pltpu.VMEM((nrow, 1), F32)],
    )
    return pl.pallas_call(
        functools.partial(_fox_sample_kernel, pages_per_step=pp, n_new=n_new),
        grid_spec=grid_spec,
        out_shape=jax.ShapeDtypeStruct((nb, n_new, a), BF16),
        compiler_params=_cparams("parallel", "arbitrary"),
        name="fox_sample",
    )(pt, qb, knt, vnt, lfnt, *([ckt] * pp), *([cvt] * pp), *([clt] * pp))


def fox_sample_attention_flat(qb, kb_new, vb_new, logf_new, cache_k, cache_v, cache_logf, page_table,
                              *, pages_per_step):
    nb, n_new, a = qb.shape
    n_pool, page, nh, hd = cache_k.shape
    n_pages = page_table.shape[1]
    pp = pages_per_step
    nrow = n_new * nh
    assert page * nh == N_HEADS * LANES and nrow <= LANES
    q32 = qb.reshape(nb, nrow, hd)
    pad = ((0, 0), (0, LANES - nrow), (0, 0))
    kn = jnp.pad(kb_new.reshape(nb, nrow, hd), pad)
    vn = jnp.pad(vb_new.reshape(nb, nrow, hd), pad)
    lfn = jnp.pad(logf_new.reshape(nb, 1, nrow), ((0, 0), (0, 0), (0, LANES - nrow)))
    lf_flat = cache_logf.reshape(n_pool, N_HEADS, LANES)
    pt = page_table.reshape(-1).astype(jnp.int32)

    def page_map(pi, ndim):
        return lambda b, g, pt_ref: (pt_ref[b * n_pages + (n_pages - 1 - (g * pp + pi))],) + (0,) * (ndim - 1)

    per_b = lambda b, g, pt_ref: (b, 0, 0)
    grid_spec = pltpu.PrefetchScalarGridSpec(
        num_scalar_prefetch=1,
        grid=(nb, n_pages // pp),
        in_specs=[pl.BlockSpec((None, nrow, hd), per_b),
                  pl.BlockSpec((None, LANES, hd), per_b),
                  pl.BlockSpec((None, LANES, hd), per_b),
                  pl.BlockSpec((None, 1, LANES), per_b)]
                 + [pl.BlockSpec((None, page, nh, hd), page_map(pi, 4)) for pi in range(pp)]
                 + [pl.BlockSpec((None, page, nh, hd), page_map(pi, 4)) for pi in range(pp)]
                 + [pl.BlockSpec((None, N_HEADS, LANES), page_map(pi, 3)) for pi in range(pp)],
        out_specs=pl.BlockSpec((None, nrow, hd), per_b),
        scratch_shapes=[pltpu.VMEM((nrow, 1), F32), pltpu.VMEM((nrow, 1), F32),
                        pltpu.VMEM((nrow, hd), F32), pltpu.VMEM((1, LANES), F32),
                        pltpu.VMEM((nrow, 1), F32)],
    )
    out = pl.pallas_call(
        functools.partial(_fox_sample_kernel, pages_per_step=pp),
        grid_spec=grid_spec,
        out_shape=jax.ShapeDtypeStruct((nb, nrow, hd), BF16),
        compiler_params=_cparams("parallel", "arbitrary"),
        name="fox_sample",
    )(pt, q32, kn, vn, lfn, *([cache_k] * pp), *([cache_v] * pp), *([lf_flat] * pp))
    return out.reshape(nb, n_new, a)


QROWS_PAD = 16


def _fox_sample_kernel_v1(pt_ref, q_ref, kn_ref, vn_ref, lfn_ref, usup_ref, *rest, pages_per_step, n_new):
    pp = pages_per_step
    k_refs = rest[:pp]
    v_refs = rest[pp:2 * pp]
    lf_refs = rest[2 * pp:3 * pp]
    o_ref, m_ref, l_ref, acc_ref, carry_ref, qbd_ref, ccol_ref = rest[3 * pp:]
    g = pl.program_id(1)
    nrow = n_new * N_HEADS
    width = q_ref.shape[1]

    sub_h = lax.broadcasted_iota(jnp.int32, (N_HEADS, width), 0)
    lane_h = lax.broadcasted_iota(jnp.int32, (N_HEADS, width), 1) // HEAD_DIM
    headmask = sub_h == lane_h
    e_sel = (lax.broadcasted_iota(jnp.int32, (nrow, N_HEADS), 0) % N_HEADS
             == lax.broadcasted_iota(jnp.int32, (nrow, N_HEADS), 1))
    e_f32 = e_sel.astype(F32)
    e_bf = e_sel.astype(BF16)

    def new_cumsum_rows():
        lfn = lfn_ref[...]
        rows, run = [], None
        for t in range(n_new):
            run = lfn[t:t + 1, :] if run is None else run + lfn[t:t + 1, :]
            rows.append(run)
        return rows

    @pl.when(g == 0)
    def _():
        m_ref[...] = jnp.full_like(m_ref, NEG_INF)
        l_ref[...] = jnp.zeros_like(l_ref)
        acc_ref[...] = jnp.zeros_like(acc_ref)
        carry_ref[...] = jnp.zeros_like(carry_ref)
        q = q_ref[...].astype(F32)
        cn = new_cumsum_rows()
        for qi in range(n_new):
            qrow = jnp.broadcast_to(q[qi:qi + 1, :], (N_HEADS, width))
            qbd_ref[qi * N_HEADS:(qi + 1) * N_HEADS, :] = jnp.where(headmask, qrow, 0.0).astype(BF16)
            ccol_ref[qi * N_HEADS:(qi + 1) * N_HEADS, :] = jnp.sum(
                e_f32[:N_HEADS, :] * cn[qi], axis=1, keepdims=True)

    def online_update(s, v_bf):
        m_old = m_ref[...]
        m_new = jnp.maximum(m_old, jnp.max(s, axis=1, keepdims=True))
        alpha = jnp.exp(m_old - m_new)
        p = jnp.exp(s - m_new)
        l_ref[...] = alpha * l_ref[...] + jnp.sum(p, axis=1, keepdims=True)
        m_ref[...] = m_new
        acc_ref[...] = alpha * acc_ref[...] + _dot(p.astype(BF16), v_bf)

    qbd = qbd_ref[...]
    ccol = ccol_ref[...]
    usup = usup_ref[...]
    for pi in range(pp):
        kb = k_refs[pi][...].astype(BF16)
        s = _dot_nt(qbd, kb)
        lf = lf_refs[pi][...]
        hi, mid, lo = _split3(lf)
        rs = _dot(usup, hi) + _dot(usup, mid) + _dot(usup, lo) + carry_ref[...]
        r_hi, r_mid, r_lo = _split3(rs)
        bias = _dot_nt(e_bf, r_hi) + _dot_nt(e_bf, r_mid) + _dot_nt(e_bf, r_lo)
        carry_ref[...] = carry_ref[...] + jnp.sum(lf, axis=0, keepdims=True)
        online_update(s + bias + ccol, v_refs[pi][...].astype(BF16))

    @pl.when(g == pl.num_programs(1) - 1)
    def _():
        cn = new_cumsum_rows()
        s = _dot_nt(qbd, kn_ref[...])
        kj = lax.broadcasted_iota(jnp.int32, (nrow, QROWS_PAD), 1)
        qi = lax.broadcasted_iota(jnp.int32, (nrow, QROWS_PAD), 0) // N_HEADS
        bias = jnp.zeros((nrow, QROWS_PAD), F32)
        for t in range(n_new):
            col = jnp.sum(e_f32 * cn[t], axis=1, keepdims=True)
            bias = jnp.where(kj == t, col, bias)
        s = s + ccol - bias
        s = jnp.where(kj <= qi, s, NEG_INF)
        online_update(s, vn_ref[...])
        o = acc_ref[...] / l_ref[...]
        for q_i in range(n_new):
            blk = o[q_i * N_HEADS:(q_i + 1) * N_HEADS, :]
            o_ref[q_i:q_i + 1, :] = jnp.sum(jnp.where(headmask, blk, 0.0), axis=0,
                                            keepdims=True).astype(o_ref.dtype)


def fox_sample_attention_v1(qb, kb_new, vb_new, logf_new, cache_k, cache_v, cache_logf, page_table,
                            *, pages_per_step):
    nb, n_new, a = qb.shape
    n_pool, page, _, _ = cache_k.shape
    n_pages = page_table.shape[1]
    pp = pages_per_step
    ck = cache_k.reshape(n_pool, page, a)
    cv = cache_v.reshape(n_pool, page, a)
    pad = ((0, 0), (0, QROWS_PAD - n_new), (0, 0))
    kn = jnp.pad(kb_new, pad)
    vn = jnp.pad(vb_new, pad)
    usup = jnp.triu(jnp.ones((page, page), F32), 1).astype(BF16)
    pt = page_table.reshape(-1).astype(jnp.int32)

    def page_map(pi):
        return lambda b, g, pt_ref: (pt_ref[b * n_pages + (n_pages - 1 - (g * pp + pi))], 0, 0)

    per_b = lambda b, g, pt_ref: (b, 0, 0)
    grid_spec = pltpu.PrefetchScalarGridSpec(
        num_scalar_prefetch=1,
        grid=(nb, n_pages // pp),
        in_specs=[pl.BlockSpec((None, n_new, a), per_b),
                  pl.BlockSpec((None, QROWS_PAD, a), per_b),
                  pl.BlockSpec((None, QROWS_PAD, a), per_b),
                  pl.BlockSpec((None, n_new, N_HEADS), per_b),
                  pl.BlockSpec((page, page), lambda b, g, pt_ref: (0, 0))]
                 + [pl.BlockSpec((None, page, a), page_map(pi)) for pi in range(pp)]
                 + [pl.BlockSpec((None, page, a), page_map(pi)) for pi in range(pp)]
                 + [pl.BlockSpec((None, page, N_HEADS), page_map(pi)) for pi in range(pp)],
        out_specs=pl.BlockSpec((None, n_new, a), per_b),
        scratch_shapes=[pltpu.VMEM((n_new * N_HEADS, 1), F32), pltpu.VMEM((n_new * N_HEADS, 1), F32),
                        pltpu.VMEM((n_new * N_HEADS, a), F32), pltpu.VMEM((1, N_HEADS), F32),
                        pltpu.VMEM((n_new * N_HEADS, a), BF16), pltpu.VMEM((n_new * N_HEADS, 1), F32)],
    )
    return pl.pallas_call(
        functools.partial(_fox_sample_kernel, pages_per_step=pp, n_new=n_new),
        grid_spec=grid_spec,
        out_shape=jax.ShapeDtypeStruct((nb, n_new, a), BF16),
        compiler_params=_cparams("parallel", "arbitrary"),
        name="fox_sample",
    )(pt, qb, kn, vn, logf_new, usup, *([ck] * pp), *([cv] * pp), *([cache_logf] * pp))


def _layer_norm(x, g, b):
    mu = jnp.mean(x, axis=-1, keepdims=True)
    xc = x - mu
    var = jnp.mean(xc * xc, axis=-1, keepdims=True)
    return xc * lax.rsqrt(var + LN_EPS) * g + b


SUBLANES = 8


def _to_row_tiles(x, ref):
    rows = x.shape[0]
    for c in range(SUBLANES):
        ref[pl.ds(c, rows, stride=SUBLANES), :] = x[:, c * LANES:(c + 1) * LANES]


def _from_row_tiles(ref, first_row, rows):
    return jnp.concatenate(
        [ref[pl.ds(first_row * SUBLANES + c, rows, stride=SUBLANES), :] for c in range(SUBLANES)], axis=1)


def _outproj_router_kernel(*refs, n_experts, aliased, n_tiles):
    if aliased:
        refs = refs[2:]
    ins, (h_ref, ht_ref, idx_ref, w_ref, rank_ref, cnt_out_ref, cnt_scr) = refs[:11], refs[11:]
    cnt_in_ref = ins[10]
    i = pl.program_id(0)

    @pl.when(i == 0)
    def _():
        cnt_scr[...] = cnt_in_ref[...]

    @pl.when(i < n_tiles)
    def _():
        _outproj_router_tile(*ins[:10], h_ref, ht_ref, idx_ref, w_ref, rank_ref, cnt_scr, n_experts=n_experts)

    @pl.when(i >= n_tiles)
    def _():
        h_ref[...] = jnp.zeros_like(h_ref)
        ht_ref[...] = jnp.zeros_like(ht_ref)

    cnt_out_ref[...] = cnt_scr[...]


def _outproj_router_tile(att_ref, ssm_ref, x_ref, woa_ref, wos_ref, g_ref, b_ref, rw_ref, rb_ref,
                         tri_ref, h_ref, ht_ref, idx_ref, w_ref, rank_ref, cnt_scr, *, n_experts):
    tm = x_ref.shape[0]
    att = att_ref[...]
    ssm = ssm_ref[...].reshape(tm, ssm_ref.shape[-1])
    mixed = _dot(att, woa_ref[...]) + _dot(ssm, wos_ref[...])
    h = _layer_norm(DN_ALPHA * x_ref[...] + mixed, g_ref[...], b_ref[...])
    h_ref[...] = h
    _to_row_tiles(h, ht_ref)
    h_hi = h.astype(BF16)
    h_lo = (h - h_hi.astype(F32)).astype(BF16)
    logits = _dot(h_hi, rw_ref[0]) + _dot(h_lo, rw_ref[0]) + _dot(h_hi, rw_ref[1]) + rb_ref[...]
    lane = lax.broadcasted_iota(jnp.int32, logits.shape, 1)
    logits = jnp.where(lane < n_experts, logits, NEG_INF)
    vals, idxs = [], []
    for _ in range(TOP_K):
        mx = jnp.max(logits, axis=1, keepdims=True)
        ix = jnp.min(jnp.where(logits == mx, lane, LANES), axis=1, keepdims=True)
        vals.append(mx)
        idxs.append(ix)
        logits = jnp.where(lane == ix, NEG_INF, logits)
    es = [jnp.exp(v - vals[0]) for v in vals]
    denom = es[0] + es[1] + es[2] + es[3]
    lane8 = lax.broadcasted_iota(jnp.int32, (tm, idx_ref.shape[1]), 1)
    idx_out = jnp.zeros(lane8.shape, jnp.int32)
    w_out = jnp.zeros(lane8.shape, F32)
    onehot = jnp.zeros(logits.shape, F32)
    for k in range(TOP_K):
        onehot = onehot + (lane == idxs[k]).astype(F32)
    before = _dot(tri_ref[...], onehot.astype(BF16)) + cnt_scr[...]
    cnt_scr[...] = cnt_scr[...] + jnp.sum(onehot, axis=0, keepdims=True)
    rank_out = jnp.zeros(lane8.shape, jnp.int32)
    for k in range(TOP_K):
        idx_out = jnp.where(lane8 == k, idxs[k], idx_out)
        w_out = jnp.where(lane8 == k, es[k] / denom, w_out)
        rank_k = jnp.sum(jnp.where(lane == idxs[k], before, 0.0), axis=1, keepdims=True)
        rank_out = jnp.where(lane8 == k, rank_k.astype(jnp.int32), rank_out)
    idx_ref[...] = idx_out
    w_ref[...] = w_out
    rank_ref[...] = rank_out


ROUTE_COLS = 8


def outproj_router(att, ssm, x2, w_out, ln_g, ln_b, router_w, router_b, counts_in, *, tm, m_total,
                   row_block_offset, h_all, ssm_chunked):
    m, d = x2.shape
    a = att.shape[1]
    n_experts = router_w.shape[1]
    nt = m // tm
    woa = w_out[:a].astype(BF16)
    wos = w_out[a:].astype(BF16)
    rw = jnp.pad(router_w.astype(F32), ((0, 0), (0, LANES - n_experts)))
    rw_hi = rw.astype(BF16)
    rw_parts = jnp.stack([rw_hi, (rw - rw_hi.astype(F32)).astype(BF16)])
    rb = jnp.pad(router_b.astype(F32), (0, LANES - n_experts)).reshape(1, LANES)
    aliased = h_all is not None
    n_fill = 0 if aliased else m_total // tm - nt - row_block_offset
    clamp = lambda i: jnp.minimum(i, nt - 1)
    row = lambda i: (clamp(i), 0)
    full = lambda i: (0, 0)
    if ssm_chunked is None:
        ssm_spec = pl.BlockSpec((tm, ssm.shape[-1]), row)
    else:
        nseq, slen, L = ssm_chunked
        tps = slen // tm
        ssm_spec = pl.BlockSpec((tm // L, None, L, ssm.shape[-1]),
                                lambda i: (clamp(i) % tps, clamp(i) // tps, 0, 0))
    in_specs = [pl.BlockSpec((tm, a), row), ssm_spec, pl.BlockSpec((tm, d), row),
                pl.BlockSpec((a, d), full), pl.BlockSpec((w_out.shape[0] - a, d), full),
                pl.BlockSpec((1, d), full), pl.BlockSpec((1, d), full),
                pl.BlockSpec((2, d, LANES), lambda i: (0, 0, 0)), pl.BlockSpec((1, LANES), full),
                pl.BlockSpec((tm, tm), full), pl.BlockSpec((1, LANES), full)]
    tri_strict = jnp.tril(jnp.ones((tm, tm), F32), -1).astype(BF16)
    args = [att, ssm, x2, woa, wos, ln_g.astype(F32).reshape(1, d), ln_b.astype(F32).reshape(1, d), rw_parts, rb,
            tri_strict, counts_in]
    assert d == SUBLANES * LANES
    aliases = {}
    if aliased:
        in_specs = [pl.BlockSpec(memory_space=pl.ANY)] * 2 + in_specs
        args = list(h_all) + args
        aliases = {0: 0, 1: 1}
    return pl.pallas_call(
        functools.partial(_outproj_router_kernel, n_experts=n_experts, aliased=aliased, n_tiles=nt),
        grid=(nt + n_fill,),
        in_specs=in_specs,
        out_specs=[pl.BlockSpec((tm, d), lambda i: (i + row_block_offset, 0)),
                   pl.BlockSpec((tm * SUBLANES, LANES), lambda i: (i + row_block_offset, 0)),
                   pl.BlockSpec((tm, ROUTE_COLS), row), pl.BlockSpec((tm, ROUTE_COLS), row),
                   pl.BlockSpec((tm, ROUTE_COLS), row), pl.BlockSpec((1, LANES), full)],
        out_shape=[jax.ShapeDtypeStruct((m_total, d), F32),
                   jax.ShapeDtypeStruct((m_total * SUBLANES, LANES), F32),
                   jax.ShapeDtypeStruct((m, ROUTE_COLS), jnp.int32),
                   jax.ShapeDtypeStruct((m, ROUTE_COLS), F32),
                   jax.ShapeDtypeStruct((m, ROUTE_COLS), jnp.int32),
                   jax.ShapeDtypeStruct((1, LANES), F32)],
        input_output_aliases=aliases,
        scratch_shapes=[pltpu.VMEM((1, LANES), F32)],
        compiler_params=_cparams("arbitrary"),
        name="outproj_router",
    )(*args)


DMA_UNROLL = 8


def _for_each_row(n_rows, fn):
    def body(j, carry):
        for u in range(DMA_UNROLL):
            fn(j * DMA_UNROLL + u, u)
        return carry
    lax.fori_loop(0, n_rows // DMA_UNROLL, body, 0)


def _tile_rows(ref, row_tile):
    return ref.at[pl.ds(pl.multiple_of(row_tile * SUBLANES, SUBLANES), SUBLANES), :]


def _scatter_rows_kernel(base_ref, ztile_ref, idx_ref, rank_ref, ht_ref, zeros_ref, xs_ref, sem, zsem, *, tile):
    n_assign = idx_ref.shape[1]
    n_experts = base_ref.shape[0]

    @pl.when(pl.program_id(0) == 0)
    def _():
        def zero_copy(e):
            row0 = pl.multiple_of(ztile_ref[e] * SUBLANES, tile * SUBLANES)
            return pltpu.make_async_copy(zeros_ref, xs_ref.at[pl.ds(row0, tile * SUBLANES), :], zsem)
        for e in range(n_experts):
            zero_copy(e).start()
        for e in range(n_experts):
            zero_copy(e).wait()

    def row_copy(a, dst_row):
        tok = lax.shift_right_logical(a, 2)
        return pltpu.make_async_copy(_tile_rows(ht_ref, tok), _tile_rows(xs_ref, dst_row), sem)

    def start(a, u):
        row_copy(a, base_ref[idx_ref[0, a]] + rank_ref[0, a]).start(priority=u % 2)

    def wait(a, u):
        row_copy(a, 0).wait()

    _for_each_row(n_assign, start)
    _for_each_row(n_assign, wait)


def scatter_rows(h_tiles, idx_flat, rank_flat, base, ztile, *, n_rows_out, tokens_per_tile, expert_tile):
    assert TOP_K == 4
    m = h_tiles.shape[0] // SUBLANES
    nt = m // tokens_per_tile
    na = tokens_per_tile * TOP_K
    grid_spec = pltpu.PrefetchScalarGridSpec(
        num_scalar_prefetch=2,
        grid=(nt,),
        in_specs=[pl.BlockSpec((None, 1, na), lambda i, b, z: (i, 0, 0), memory_space=pltpu.SMEM),
                  pl.BlockSpec((None, 1, na), lambda i, b, z: (i, 0, 0), memory_space=pltpu.SMEM),
                  pl.BlockSpec((tokens_per_tile * SUBLANES, LANES), lambda i, b, z: (i, 0)),
                  pl.BlockSpec((expert_tile * SUBLANES, LANES), lambda i, b, z: (0, 0))],
        out_specs=pl.BlockSpec(memory_space=pl.ANY),
        scratch_shapes=[pltpu.SemaphoreType.DMA, pltpu.SemaphoreType.DMA],
    )
    return pl.pallas_call(
        functools.partial(_scatter_rows_kernel, tile=expert_tile),
        grid_spec=grid_spec,
        out_shape=jax.ShapeDtypeStruct((n_rows_out * SUBLANES, LANES), h_tiles.dtype),
        compiler_params=_cparams("arbitrary"),
        name="moe_scatter",
    )(base, ztile, idx_flat.reshape(nt, 1, na), rank_flat.reshape(nt, 1, na), h_tiles,
      jnp.zeros((expert_tile * SUBLANES, LANES), h_tiles.dtype))


def _expert_kernel(te_ref, tv_ref, x_ref, wup_ref, bup_ref, wdn_ref, bdn_ref, y_ref, wup_bf, wdn_bf):
    i = pl.program_id(0)
    de = wdn_ref.shape[0]
    tm = x_ref.shape[0] // SUBLANES

    @pl.when((i == 0) | (te_ref[i] != te_ref[jnp.maximum(i - 1, 0)]))
    def _():
        wup_bf[...] = wup_ref[...].astype(BF16)
        wdn_bf[...] = wdn_ref[...].astype(BF16)

    @pl.when(tv_ref[i] > 0)
    def _():
        xb = _from_row_tiles(x_ref, 0, tm).astype(BF16)
        hcat = _dot(xb, wup_bf[...]) + bup_ref[...]
        x_glu = jnp.minimum(hcat[:, :de], SWIGLU_LIMIT)
        x_lin = jnp.clip(hcat[:, de:], -SWIGLU_LIMIT, SWIGLU_LIMIT)
        act = x_glu * jax.nn.sigmoid(SWIGLU_ALPHA * x_glu) * (x_lin + 1.0)
        _to_row_tiles(_dot(act.astype(BF16), wdn_bf[...]) + bdn_ref[...], y_ref)

    @pl.when(tv_ref[i] == 0)
    def _():
        y_ref[...] = jnp.zeros_like(y_ref)


def expert_mlp(xs, tile_expert, tile_valid, w_up, b_up, w_down, b_down, *, tm):
    npad = xs.shape[0] // SUBLANES
    n_e, d, d2 = w_up.shape
    de = w_down.shape[1]
    dout = w_down.shape[2]
    assert d == SUBLANES * LANES and dout == SUBLANES * LANES
    nt = npad // tm
    grid_spec = pltpu.PrefetchScalarGridSpec(
        num_scalar_prefetch=2,
        grid=(nt,),
        in_specs=[pl.BlockSpec((tm * SUBLANES, LANES), lambda i, te, tv: (jnp.where(tv[i] > 0, i, te[nt]), 0)),
                  pl.BlockSpec((None, d, d2), lambda i, te, tv: (te[i], 0, 0)),
                  pl.BlockSpec((None, 1, d2), lambda i, te, tv: (te[i], 0, 0)),
                  pl.BlockSpec((None, de, dout), lambda i, te, tv: (te[i], 0, 0)),
                  pl.BlockSpec((None, 1, dout), lambda i, te, tv: (te[i], 0, 0))],
        out_specs=pl.BlockSpec((tm * SUBLANES, LANES), lambda i, te, tv: (i, 0)),
        scratch_shapes=[pltpu.VMEM((d, d2), BF16), pltpu.VMEM((de, dout), BF16)],
    )
    return pl.pallas_call(
        _expert_kernel,
        grid_spec=grid_spec,
        out_shape=jax.ShapeDtypeStruct((npad * SUBLANES, LANES), F32),
        compiler_params=_cparams("arbitrary"),
        name="moe_experts",
    )(tile_expert, tile_valid, xs, w_up.astype(F32),
      b_up.astype(F32).reshape(n_e, 1, d2), w_down.astype(F32), b_down.astype(F32).reshape(n_e, 1, dout))


def _combine_ln_kernel(base_ref, idx_ref, rank_ref, idxn_ref, rankn_ref, h_ref, w_ref, g_ref, b_ref, ys_ref,
                       o_ref, buf0, buf1, sem0, sem1):
    tm = h_ref.shape[0]
    n_assign = idx_ref.shape[1]
    i = pl.program_id(0)
    last = pl.num_programs(0) - 1

    def gather(idx_r, rank_r, buf, sem):
        def row_copy(a, src_row):
            dst = lax.bitwise_and(a, TOP_K - 1) * tm + lax.shift_right_logical(a, 2)
            return pltpu.make_async_copy(_tile_rows(ys_ref, src_row), _tile_rows(buf, dst), sem)

        def start(a, u):
            row_copy(a, base_ref[idx_r[0, a]] + rank_r[0, a]).start(priority=u % 2)

        def wait(a, u):
            row_copy(a, 0).wait()

        return start, wait

    def step(buf_cur, sem_cur, buf_nxt, sem_nxt):
        start_cur, wait_cur = gather(idx_ref, rank_ref, buf_cur, sem_cur)
        start_nxt, _ = gather(idxn_ref, rankn_ref, buf_nxt, sem_nxt)

        @pl.when(i == 0)
        def _():
            _for_each_row(n_assign, start_cur)

        @pl.when(i < last)
        def _():
            _for_each_row(n_assign, start_nxt)

        _for_each_row(n_assign, wait_cur)
        w = w_ref[...]
        moe = w[:, 0:1] * _from_row_tiles(buf_cur, 0, tm)
        for k in range(1, TOP_K):
            moe = moe + w[:, k:k + 1] * _from_row_tiles(buf_cur, k * tm, tm)
        o_ref[...] = _layer_norm(DN_ALPHA * h_ref[...] + moe, g_ref[...], b_ref[...])

    @pl.when(i % 2 == 0)
    def _():
        step(buf0, sem0, buf1, sem1)

    @pl.when(i % 2 == 1)
    def _():
        step(buf1, sem1, buf0, sem0)


def combine_ln(h_all, ys, idx_flat, rank_flat, top_w, base, ln_g, ln_b, *, tm, m, row_block_offset):
    assert TOP_K == 4
    d = h_all.shape[1]
    nt = m // tm
    na = tm * TOP_K
    grid_spec = pltpu.PrefetchScalarGridSpec(
        num_scalar_prefetch=1,
        grid=(nt,),
        in_specs=[pl.BlockSpec((None, 1, na), lambda i, b: (i, 0, 0), memory_space=pltpu.SMEM),
                  pl.BlockSpec((None, 1, na), lambda i, b: (i, 0, 0), memory_space=pltpu.SMEM),
                  pl.BlockSpec((None, 1, na), lambda i, b: (jnp.minimum(i + 1, nt - 1), 0, 0), memory_space=pltpu.SMEM),
                  pl.BlockSpec((None, 1, na), lambda i, b: (jnp.minimum(i + 1, nt - 1), 0, 0), memory_space=pltpu.SMEM),
                  pl.BlockSpec((tm, d), lambda i, b: (i + row_block_offset, 0)),
                  pl.BlockSpec((tm, ROUTE_COLS), lambda i, b: (i, 0)),
                  pl.BlockSpec((1, d), lambda i, b: (0, 0)),
                  pl.BlockSpec((1, d), lambda i, b: (0, 0)),
                  pl.BlockSpec(memory_space=pl.ANY)],
        out_specs=pl.BlockSpec((tm, d), lambda i, b: (i, 0)),
        scratch_shapes=[pltpu.VMEM((TOP_K * tm * SUBLANES, LANES), F32),
                        pltpu.VMEM((TOP_K * tm * SUBLANES, LANES), F32),
                        pltpu.SemaphoreType.DMA, pltpu.SemaphoreType.DMA],
    )
    idx3 = idx_flat.reshape(nt, 1, na)
    rank3 = rank_flat.reshape(nt, 1, na)
    return pl.pallas_call(
        _combine_ln_kernel,
        grid_spec=grid_spec,
        out_shape=jax.ShapeDtypeStruct((m, d), F32),
        compiler_params=_cparams("arbitrary"),
        name="moe_combine_ln",
    )(base, idx3, rank3, idx3, rank3, h_all, top_w,
      ln_g.astype(F32).reshape(1, d), ln_b.astype(F32).reshape(1, d), ys)


def _tile_plan(counts, n_assign, tm):
    n_experts = counts.shape[0]
    n_tiles = n_assign // tm + n_experts
    ntile_e = (counts + tm - 1) // tm
    tile_end = jnp.cumsum(ntile_e)
    tile_start = tile_end - ntile_e
    tile_ids = jnp.arange(n_tiles, dtype=jnp.int32)
    tile_expert = jnp.minimum((tile_ids[:, None] >= tile_end[None, :]).sum(axis=1), n_experts - 1)
    n_used = tile_end[-1]
    tile_valid = (tile_ids < n_used).astype(jnp.int32)
    tile_expert = jnp.concatenate([tile_expert, jnp.maximum(n_used - 1, 0)[None]]).astype(jnp.int32)
    base = (tile_start * tm).astype(jnp.int32)
    ztile = jnp.where(ntile_e > 0, (tile_end - 1) * tm, (n_tiles - 1) * tm).astype(jnp.int32)
    return n_tiles, tile_expert, tile_valid, base, ztile


S5_CHUNK = 16
TOKEN_TILE = 512
ATTN_TILE = 1024
S5_CHUNKS_PER_TILE = 16
PAGES_PER_STEP = 32
EXPERT_TILE = 512
SCATTER_TILE = 512
COMBINE_TILE = 256


def kernel(x_prompt, x_sample, cache_k, cache_v, cache_logf, state_ssm_re, state_ssm_im, page_table, w_in, b_forget, ssm_a_re, ssm_a_im, ssm_log_dt, ssm_b_re, ssm_b_im, ssm_c_re, ssm_c_im, ssm_d, ssm_glu_w, ssm_glu_b, w_out, ln1_g, ln1_b, router_w, router_b, w_up, b_up, w_down, b_down, ln2_g, ln2_b):
    bsz, slen, d = x_prompt.shape
    dec_b, dec_t, _ = x_sample.shape
    m_p = bsz * slen
    m_s = dec_b * dec_t
    m_all = m_p + m_s
    n_groups, n_state = ssm_a_re.shape
    n_experts = router_w.shape[1]
    tm = TOKEN_TILE
    assert m_s == tm and slen % tm == 0 and m_p % tm == 0
    assert (m_all * TOP_K) % EXPERT_TILE == 0 and m_p % COMBINE_TILE == 0 and m_s % COMBINE_TILE == 0

    qb, k_p, v_p, kb, vb, logf_p, u_p = in_projection(x_prompt, w_in, b_forget, tm=tm, chunk_len=S5_CHUNK,
                                                      aug=True)
    att_p = fox_prompt_attention(qb, kb, vb, bsz, slen, tq=ATTN_TILE)
    tables_p = _s5_tables(ssm_a_re, ssm_a_im, ssm_log_dt, ssm_b_re, ssm_b_im, ssm_c_re, ssm_c_im, S5_CHUNK)
    zeros_state = jnp.zeros((bsz, n_groups, n_state), F32)
    ssm_p, hre_p, him_p = s5_mixer(u_p.reshape(m_p, u_p.shape[-1]), zeros_state, zeros_state, tables_p,
                                   ssm_d, ssm_glu_w, ssm_glu_b, nseq=bsz, chunk_len=S5_CHUNK,
                                   chunks_per_tile=S5_CHUNKS_PER_TILE, precise_state=False)

    qs, k_s, v_s, ksb, vsb, logf_s, u_s = in_projection(x_sample.reshape(1, m_s, d), w_in, b_forget,
                                                         tm=tm, chunk_len=None, aug=False)
    a = ATTN_WIDTH
    att_s = fox_sample_attention(qs.reshape(dec_b, dec_t, a), ksb.reshape(dec_b, dec_t, a),
                                 vsb.reshape(dec_b, dec_t, a), logf_s.reshape(dec_b, dec_t, N_HEADS),
                                 cache_k, cache_v, cache_logf, page_table, pages_per_step=PAGES_PER_STEP)
    tables_s = _s5_tables(ssm_a_re, ssm_a_im, ssm_log_dt, ssm_b_re, ssm_b_im, ssm_c_re, ssm_c_im, dec_t)
    ssm_s, hre_s, him_s = s5_mixer(u_s, state_ssm_re, state_ssm_im, tables_s, ssm_d, ssm_glu_w, ssm_glu_b,
                                   nseq=dec_b, chunk_len=dec_t, chunks_per_tile=1, precise_state=True)

    ssm_p4 = ssm_p.reshape(slen // S5_CHUNK, bsz, S5_CHUNK, ssm_p.shape[-1])
    zero_counts = jnp.zeros((1, LANES), F32)
    h_all, h_tiles, idx_p, w_p, rank_p, cnt_p = outproj_router(
        att_p, ssm_p4, x_prompt.reshape(m_p, d), w_out, ln1_g, ln1_b, router_w, router_b, zero_counts,
        tm=tm, m_total=m_all, row_block_offset=0, h_all=None, ssm_chunked=(bsz, slen, S5_CHUNK))
    h_all, h_tiles, idx_s, w_s, rank_s, cnt_all = outproj_router(
        att_s.reshape(m_s, a), ssm_s, x_sample.reshape(m_s, d), w_out, ln1_g, ln1_b, router_w, router_b, cnt_p,
        tm=tm, m_total=m_all, row_block_offset=m_p // tm, h_all=(h_all, h_tiles), ssm_chunked=None)

    counts = cnt_all[0, :n_experts].astype(jnp.int32)
    n_tiles, tile_expert, tile_valid, base, ztile = _tile_plan(counts, m_all * TOP_K, EXPERT_TILE)
    flat = lambda t: t[:, :TOP_K].reshape(-1)
    idx_pf, idx_sf, rank_pf, rank_sf = flat(idx_p), flat(idx_s), flat(rank_p), flat(rank_s)
    xs = scatter_rows(h_tiles, jnp.concatenate([idx_pf, idx_sf]), jnp.concatenate([rank_pf, rank_sf]), base, ztile,
                      n_rows_out=n_tiles * EXPERT_TILE, tokens_per_tile=SCATTER_TILE, expert_tile=EXPERT_TILE)
    ys = expert_mlp(xs, tile_expert, tile_valid, w_up, b_up, w_down, b_down, tm=EXPERT_TILE)
    y_p = combine_ln(h_all, ys, idx_pf, rank_pf, w_p, base, ln2_g, ln2_b, tm=COMBINE_TILE, m=m_p,
                     row_block_offset=0)
    y_s = combine_ln(h_all, ys, idx_sf, rank_sf, w_s, base, ln2_g, ln2_b, tm=COMBINE_TILE, m=m_s,
                     row_block_offset=m_p // COMBINE_TILE)

    hd = (N_HEADS, HEAD_DIM)
    return (y_p.reshape(bsz, slen, d), y_s.reshape(dec_b, dec_t, d),
            k_p.reshape(bsz, *hd, slen).transpose(0, 3, 1, 2), v_p.reshape(bsz, *hd, slen).transpose(0, 3, 1, 2),
            logf_p.reshape(bsz, slen, N_HEADS).astype(cache_logf.dtype),
            hre_p.astype(state_ssm_re.dtype), him_p.astype(state_ssm_im.dtype),
            k_s.reshape(dec_b, dec_t, *hd), v_s.reshape(dec_b, dec_t, *hd),
            logf_s.reshape(dec_b, dec_t, N_HEADS).astype(cache_logf.dtype),
            hre_s.astype(state_ssm_re.dtype), him_s.astype(state_ssm_im.dtype))
```

```python
import functools
import math

import jax
import jax.numpy as jnp
import numpy as np
from jax import lax
from jax.experimental import pallas as pl
from jax.experimental.pallas import tpu as pltpu

N_HEADS = 8
HEAD_DIM = 64
ATTN_WIDTH = N_HEADS * HEAD_DIM
SSM_GROUP = 16
SSM_STATE = 64
TOP_K = 4
SWIGLU_LIMIT = 7.0
SWIGLU_ALPHA = 1.702
LN_EPS = 1e-5
DEPTH = 1
DN_ALPHA = (2.0 * DEPTH) ** 0.25
ATTN_SCALE = HEAD_DIM ** -0.5

LANES = 128
GROUPS_PER_LANE_BLOCK = LANES // SSM_GROUP
STATE_LANES = GROUPS_PER_LANE_BLOCK * 2 * SSM_STATE
VMEM_LIMIT_BYTES = 56 * 1024 * 1024

BF16 = jnp.bfloat16
F32 = jnp.float32
NEG_INF = float("-inf")


def _cparams(*sem):
    return pltpu.CompilerParams(dimension_semantics=sem, vmem_limit_bytes=VMEM_LIMIT_BYTES)


def _split3(x):
    hi = x.astype(BF16)
    r1 = x - hi.astype(F32)
    mid = r1.astype(BF16)
    lo = (r1 - mid.astype(F32)).astype(BF16)
    return hi, mid, lo


def _dot(a, b):
    return jnp.dot(a, b, preferred_element_type=F32)


def _dot_nt(a, b):
    return lax.dot_general(a, b, (((1,), (1,)), ((), ())), preferred_element_type=F32)


LOG2E = math.log2(math.e)
AUG_C_LANE = HEAD_DIM
AUG_ONE_LANE = HEAD_DIM + 3


def _aug_tables():
    e = np.zeros((LANES, 2 * N_HEADS * LANES), np.float32)
    ones = np.zeros((1, 2 * N_HEADS * LANES), np.float32)
    koff = N_HEADS * LANES
    for h in range(N_HEADS):
        for p in range(3):
            src = p * N_HEADS + h
            e[src, h * LANES + AUG_C_LANE + p] = 1.0
            e[src, koff + h * LANES + AUG_ONE_LANE + p] = -1.0
            ones[0, h * LANES + AUG_ONE_LANE + p] = 1.0
            ones[0, koff + h * LANES + AUG_C_LANE + p] = 1.0
    return jnp.asarray(e, BF16), jnp.asarray(ones)


def _tri_ones(n, k):
    return jnp.asarray(np.tril(np.ones((n, n), np.float32), k), BF16)


def _spread_heads(x):
    low = lax.broadcasted_iota(jnp.int32, (1, LANES), 1) < HEAD_DIM
    blocks = []
    for j in range(x.shape[1] // LANES):
        pair = x[:, j * LANES:(j + 1) * LANES]
        blocks.append(jnp.where(low, pair, 0.0))
        blocks.append(jnp.where(low, pltpu.roll(pair, HEAD_DIM, 1), 0.0))
    return jnp.concatenate(blocks, axis=1)


def _inproj_kernel(x_ref, wqkv_ref, wf_ref, wu_ref, bf_ref, *rest, chunk_len, aug):
    if aug:
        tri_ref, e_ref, ones_ref, q_ref, k_ref, v_ref, kb_ref, vb_ref, logf_ref, u_ref, carry_ref = rest
    else:
        q_ref, k_ref, v_ref, kb_ref, vb_ref, logf_ref, u_ref = rest
    xb = x_ref[...].astype(BF16)
    qkv = _dot(xb, wqkv_ref[...])
    a = ATTN_WIDTH
    q = qkv[:, :a]
    k = qkv[:, a:2 * a]
    v = qkv[:, 2 * a:]
    if aug:
        k_ref[...] = k.T
        v_ref[...] = v.T
    else:
        k_ref[...] = k
        v_ref[...] = v
    vb_ref[...] = v.astype(BF16)
    f = _dot(xb, wf_ref[...]) + bf_ref[...]
    logf = jnp.minimum(f, 0.0) - jnp.log1p(jnp.exp(-jnp.abs(f)))
    logf_ref[...] = logf[:, :N_HEADS]

    if aug:
        @pl.when(pl.program_id(1) == 0)
        def _():
            carry_ref[...] = jnp.zeros_like(carry_ref)
        hi, mid, lo = _split3(logf)
        tri = tri_ref[...]
        c = _dot(tri, hi) + _dot(tri, mid) + _dot(tri, lo) + carry_ref[...]
        carry_ref[...] = c[-1:, :]
        c_hi, c_mid, c_lo = _split3(c * LOG2E)
        lane = lax.broadcasted_iota(jnp.int32, (1, LANES), 1)
        pieces = jnp.where(lane < N_HEADS, c_hi.astype(F32),
                           jnp.where(lane < 2 * N_HEADS, pltpu.roll(c_mid.astype(F32), N_HEADS, 1),
                                     jnp.where(lane < 3 * N_HEADS, pltpu.roll(c_lo.astype(F32), 2 * N_HEADS, 1),
                                               0.0)))
        extra = _dot(pieces.astype(BF16), e_ref[...]) + ones_ref[...]
        half = N_HEADS * LANES
        q_ref[...] = (_spread_heads(q) + extra[:, :half]).astype(BF16)
        kb_ref[...] = (_spread_heads(k) + extra[:, half:]).astype(BF16)
    else:
        q_ref[...] = q.astype(BF16)
        kb_ref[...] = k.astype(BF16)

    u = _dot(xb, wu_ref[...])
    if chunk_len is None:
        u_ref[...] = u
    else:
        u_ref[...] = u.reshape(u.shape[0] // chunk_len, chunk_len, u.shape[1])


def in_projection(x, w_in, b_forget, *, tm, chunk_len, aug):
    nseq, slen, d = x.shape
    m = nseq * slen
    nt = slen // tm
    a = ATTN_WIDTH
    x2 = x.reshape(m, d)
    q_scale = ATTN_SCALE * LOG2E if aug else ATTN_SCALE
    col_scale = jnp.concatenate([jnp.full((a,), q_scale, F32), jnp.ones((2 * a,), F32)])
    wqkv = (w_in[:, :3 * a] * col_scale).astype(BF16)
    wf = jnp.pad(w_in[:, 3 * a:3 * a + N_HEADS], ((0, 0), (0, LANES - N_HEADS))).astype(BF16)
    wu = w_in[:, 3 * a + N_HEADS:].astype(BF16)
    ssm_w = wu.shape[1]
    bf = jnp.pad(b_forget.astype(F32), (0, LANES - N_HEADS)).reshape(1, LANES)

    row = lambda b, i: (b * nt + i, 0)
    full = lambda b, i: (0, 0)
    tok = lambda w, dt: jax.ShapeDtypeStruct((m, w), dt)
    if chunk_len is None:
        u_shape = tok(ssm_w, F32)
        u_spec = pl.BlockSpec((tm, ssm_w), row)
    else:
        u_shape = jax.ShapeDtypeStruct((slen // chunk_len, nseq, chunk_len, ssm_w), F32)
        u_spec = pl.BlockSpec((tm // chunk_len, None, chunk_len, ssm_w), lambda b, i: (i, b, 0, 0))
    in_specs = [pl.BlockSpec((tm, d), row),
                pl.BlockSpec((d, 3 * a), full),
                pl.BlockSpec((d, LANES), full),
                pl.BlockSpec((d, ssm_w), full),
                pl.BlockSpec((1, LANES), full)]
    args = [x2, wqkv, wf, wu, bf]
    qk_w = a
    scratch = []
    kv_spec = pl.BlockSpec((tm, a), row)
    kv_shape = tok(a, F32)
    if aug:
        kv_spec = pl.BlockSpec((None, a, tm), lambda b, i: (b, 0, i))
        kv_shape = jax.ShapeDtypeStruct((nseq, a, slen), F32)
        e_mat, ones_row = _aug_tables()
        qk_w = N_HEADS * LANES
        in_specs += [pl.BlockSpec((tm, tm), full), pl.BlockSpec((LANES, 2 * qk_w), full),
                     pl.BlockSpec((1, 2 * qk_w), full)]
        args += [_tri_ones(tm, 0), e_mat, ones_row]
        scratch = [pltpu.VMEM((1, LANES), F32)]
    return pl.pallas_call(
        functools.partial(_inproj_kernel, chunk_len=chunk_len, aug=aug),
        grid=(nseq, nt),
        in_specs=in_specs,
        out_specs=[pl.BlockSpec((tm, qk_w), row), kv_spec, kv_spec,
                   pl.BlockSpec((tm, qk_w), row), pl.BlockSpec((tm, a), row),
                   pl.BlockSpec((tm, N_HEADS), row), u_spec],
        out_shape=[tok(qk_w, BF16), kv_shape, kv_shape, tok(qk_w, BF16), tok(a, BF16),
                   tok(N_HEADS, F32), u_shape],
        scratch_shapes=scratch,
        compiler_params=_cparams("parallel", "arbitrary"),
        name="in_projection",
    )(*args)


def _fox_prompt_kernel(q_ref, k_ref, v_ref, o_ref, m_ref, l_ref, acc_ref):
    i = pl.program_id(1)
    kk = pl.program_id(2)
    nk = pl.num_programs(2)
    tq = q_ref.shape[0]
    tk = k_ref.shape[0]
    r = tq // tk

    @pl.when(kk == 0)
    def _():
        m_ref[...] = jnp.full_like(m_ref, NEG_INF)
        l_ref[...] = jnp.zeros_like(l_ref)
        acc_ref[...] = jnp.zeros_like(acc_ref)

    def step(masked):
        lane = lax.broadcasted_iota(jnp.int32, (1, LANES), 1)
        low = lane < HEAD_DIM
        if masked:
            qpos = lax.broadcasted_iota(jnp.int32, (tq, tk), 0)
            kpos = lax.broadcasted_iota(jnp.int32, (tq, tk), 1) + (kk - i * r) * tk
            causal = kpos <= qpos
        for blk in range(N_HEADS // 2):
            sl = slice(blk * LANES, (blk + 1) * LANES)
            v = v_ref[:, sl]
            pvs, alphas = [], []
            for half in range(2):
                h = 2 * blk + half
                hs = slice(h * LANES, (h + 1) * LANES)
                s = _dot_nt(q_ref[:, hs], k_ref[:, hs])
                if masked:
                    s = jnp.where(causal, s, NEG_INF)
                m_old = m_ref[:, h:h + 1]
                m_new = jnp.maximum(m_old, jnp.max(s, axis=1, keepdims=True))
                alpha = jnp.exp2(m_old - m_new)
                p = jnp.exp2(s - m_new)
                l_ref[:, h:h + 1] = alpha * l_ref[:, h:h + 1] + jnp.sum(p, axis=1, keepdims=True)
                m_ref[:, h:h + 1] = m_new
                pvs.append(_dot(p.astype(BF16), v))
                alphas.append(alpha)
            alpha2 = jnp.where(low, alphas[0], alphas[1])
            pv2 = jnp.where(low, pvs[0], pvs[1])
            acc_ref[:, sl] = alpha2 * acc_ref[:, sl] + pv2

    @pl.when(kk < i * r)
    def _():
        step(False)

    @pl.when((kk >= i * r) & (kk < (i + 1) * r))
    def _():
        step(True)

    @pl.when(kk == nk - 1)
    def _():
        lane = lax.broadcasted_iota(jnp.int32, (1, LANES), 1)
        low = lane < HEAD_DIM
        for blk in range(N_HEADS // 2):
            sl = slice(blk * LANES, (blk + 1) * LANES)
            l2 = jnp.where(low, l_ref[:, 2 * blk:2 * blk + 1], l_ref[:, 2 * blk + 1:2 * blk + 2])
            o_ref[:, sl] = (acc_ref[:, sl] / l2).astype(o_ref.dtype)


def fox_prompt_attention(q_aug, k_aug, vb, nseq, slen, *, tq, tk):
    m = nseq * slen
    nq = slen // tq
    nk = slen // tk
    r = tq // tk
    assert tq == r * tk
    a = ATTN_WIDTH
    aw = q_aug.shape[1]
    qmap = lambda b, i, kk: (b * nq + i, 0)
    kmap = lambda b, i, kk: (b * nk + jnp.minimum(kk, (i + 1) * r - 1), 0)
    return pl.pallas_call(
        _fox_prompt_kernel,
        grid=(nseq, nq, nk),
        in_specs=[pl.BlockSpec((tq, aw), qmap),
                  pl.BlockSpec((tk, aw), kmap),
                  pl.BlockSpec((tk, a), kmap)],
        out_specs=pl.BlockSpec((tq, a), qmap),
        out_shape=jax.ShapeDtypeStruct((m, a), BF16),
        scratch_shapes=[pltpu.VMEM((tq, N_HEADS), F32), pltpu.VMEM((tq, N_HEADS), F32),
                        pltpu.VMEM((tq, a), F32)],
        compiler_params=_cparams("parallel", "parallel", "arbitrary"),
        name="fox_prompt",
    )(q_aug, k_aug, vb)


def _s5_tables(a_re, a_im, log_dt, b_re, b_im, c_re, c_im, chunk_len):
    g, n = a_re.shape
    cdim = b_re.shape[2]
    nblk = g // GROUPS_PER_LANE_BLOCK
    gb = GROUPS_PER_LANE_BLOCK
    a_re = a_re.astype(F32)
    a_im = a_im.astype(F32)
    b_re = b_re.astype(F32)
    b_im = b_im.astype(F32)
    c_re = c_re.astype(F32)
    c_im = c_im.astype(F32)
    dt = jnp.exp(log_dt.astype(F32))[:, None]
    mag = jnp.exp(a_re * dt)
    ar = mag * jnp.cos(a_im * dt)
    ai = mag * jnp.sin(a_im * dt)
    den = a_re * a_re + a_im * a_im
    cr = ((ar - 1.0) * a_re + ai * a_im) / den
    ci = (ai * a_re - (ar - 1.0) * a_im) / den
    bbr = cr[..., None] * b_re - ci[..., None] * b_im
    bbi = cr[..., None] * b_im + ci[..., None] * b_re
    pr = [jnp.ones_like(ar)]
    pi = [jnp.zeros_like(ai)]
    for _ in range(chunk_len):
        pr_n = pr[-1] * ar - pi[-1] * ai
        pi_n = pr[-1] * ai + pi[-1] * ar
        pr.append(pr_n)
        pi.append(pi_n)
    pr = jnp.stack(pr)
    pi = jnp.stack(pi)
    L = chunk_len
    pbr = pr[:L, :, :, None] * bbr[None] - pi[:L, :, :, None] * bbi[None]
    pbi = pr[:L, :, :, None] * bbi[None] + pi[:L, :, :, None] * bbr[None]
    kmat = jnp.einsum('gon,lgni->lgoi', c_re, pbr) - jnp.einsum('gon,lgni->lgoi', c_im, pbi)
    eye = jnp.asarray(np.eye(gb, dtype=np.float32))
    bd = jnp.einsum('lqgoi,gh->lqgiho', kmat.reshape(L, nblk, gb, cdim, cdim), eye)
    bd = bd.reshape(L, nblk, gb * cdim, gb * cdim)
    steps = np.arange(L)
    shift = jnp.asarray((steps[None, None, :] - steps[None, :, None] == steps[:, None, None])
                        .astype(np.float32))
    tmat = jnp.einsum('lst,lqrc->qsrtc', shift, bd, precision=lax.Precision.HIGHEST)
    tmat = tmat.reshape(nblk, L * gb * cdim, L * gb * cdim)
    wr = pbr[::-1].reshape(L, nblk, gb, n, cdim)
    wi = pbi[::-1].reshape(L, nblk, gb, n, cdim)
    wri = jnp.stack([wr, wi], axis=3)
    w2 = wri.transpose(0, 1, 2, 5, 3, 4).reshape(L, nblk, gb, cdim, 2 * n)
    wmat = jnp.einsum('sqgix,gh->qsgihx', w2, eye).reshape(nblk, L * gb * cdim, gb * 2 * n)
    cpr = c_re[None] * pr[1:, :, None, :] - c_im[None] * pi[1:, :, None, :]
    cpi = c_re[None] * pi[1:, :, None, :] + c_im[None] * pr[1:, :, None, :]
    vri = jnp.stack([cpr, -cpi], axis=1).reshape(L, 2, nblk, gb, cdim, n)
    vs = vri.transpose(2, 3, 1, 5, 0, 4).reshape(nblk, gb, 2 * n, L, cdim)
    place = np.einsum('gh,op->gohp', np.eye(gb, dtype=np.float32), np.eye(cdim, dtype=np.float32))
    place = jnp.asarray(place.reshape(gb, cdim, gb * cdim))
    vmat = jnp.einsum('qgxto,gop->qgxtp', vs, place, precision=lax.Precision.HIGHEST)
    vmat = vmat.reshape(nblk, gb * 2 * n, L * gb * cdim)
    al_r = pr[L].reshape(nblk, gb, n)
    al_i = pi[L].reshape(nblk, gb, n)
    a1 = jnp.stack([al_r, al_r], axis=2).reshape(nblk, 1, gb * 2 * n)
    a2 = jnp.stack([-al_i, al_i], axis=2).reshape(nblk, 1, gb * 2 * n)
    return tmat, wmat, vmat, a1, a2


def _gelu_exact(y):
    return 0.5 * y * (1.0 + lax.erf(y * (2.0 ** -0.5)))


def _s5_kernel(u_ref, h0_ref, t_ref, w_ref, v_ref, a1_ref, a2_ref, d_ref, gw_ref, gb_ref,
               y_ref, hout_ref, h_scr, hp_scr, y_scr, *, chunk_len, nseq, precise_state):
    i = pl.program_id(1)
    L = chunk_len
    rows = u_ref.shape[0]
    n = rows // L
    nj = n // nseq

    @pl.when(i == 0)
    def _():
        h_scr[...] = h0_ref[...]

    xs = [u_ref[pl.ds(t, n, stride=L), :] for t in range(L)] if L > 1 else [u_ref[...]]
    x32 = jnp.concatenate(xs, axis=1)
    xb = x32.astype(BF16)
    w = w_ref[0]
    s = _dot(xb, w)
    if precise_state:
        xlo = (x32 - xb.astype(F32)).astype(BF16)
        s = s + _dot(xlo, w) + _dot(xb, w_ref[1])
    a1 = a1_ref[...]
    a2 = a2_ref[...]
    h = h_scr[...]
    width = h.shape[1]
    is_re = (lax.broadcasted_iota(jnp.int32, (1, width), 1) % (2 * SSM_STATE)) < SSM_STATE
    for j in range(nj):
        hp_scr[j * nseq:(j + 1) * nseq, :] = h
        swapped = jnp.where(is_re, pltpu.roll(h, width - SSM_STATE, 1), pltpu.roll(h, SSM_STATE, 1))
        h = a1 * h + a2 * swapped + s[j * nseq:(j + 1) * nseq, :]
    h_scr[...] = h
    y = _dot(xb, t_ref[...]) + _dot(hp_scr[...].astype(BF16), v_ref[...])
    if L > 1:
        for t in range(L):
            y_scr[pl.ds(t, n, stride=L), :] = y[:, t * LANES:(t + 1) * LANES]
        ytok = y_scr[...]
    else:
        ytok = y
    ytok = ytok + d_ref[...] * u_ref[...]
    g = _gelu_exact(ytok)
    z = _dot(g.astype(BF16), gw_ref[...]) + gb_ref[...]
    y_ref[...] = (g * jax.nn.sigmoid(z)).astype(y_ref.dtype)

    @pl.when(i == pl.num_programs(1) - 1)
    def _():
        hout_ref[...] = h


def s5_mixer(u_rows, h0_re, h0_im, tables, ssm_d, glu_w, glu_b, *, nseq, chunk_len, chunks_per_tile,
             precise_state):
    tmat, wmat, vmat, a1, a2 = tables
    mrows, width = u_rows.shape
    nblk = width // LANES
    gb = GROUPS_PER_LANE_BLOCK
    L = chunk_len
    rows_tile = chunks_per_tile * nseq * L
    nt = mrows // rows_tile
    g_total = h0_re.shape[1]
    n = h0_re.shape[2]
    h0 = jnp.stack([h0_re.astype(F32), h0_im.astype(F32)], axis=2)
    h0 = h0.reshape(nseq, nblk, gb * 2 * n).transpose(1, 0, 2)
    drow = ssm_d.astype(F32).reshape(nblk, 1, LANES)
    gwb = jnp.einsum('qgce,gh->qgche', glu_w.astype(F32).reshape(nblk, gb, SSM_GROUP, SSM_GROUP),
                     jnp.eye(gb, dtype=F32)).reshape(nblk, LANES, LANES).astype(BF16)
    gbrow = glu_b.astype(F32).reshape(nblk, 1, LANES)
    kl = L * LANES
    sl = STATE_LANES
    blk = lambda q, i: (q, 0, 0)
    w_hi = wmat.astype(BF16)
    if precise_state:
        w_parts = jnp.stack([w_hi, (wmat - w_hi.astype(F32)).astype(BF16)], axis=1)
    else:
        w_parts = w_hi[:, None]
    nparts = w_parts.shape[1]
    y, hout = pl.pallas_call(
        functools.partial(_s5_kernel, chunk_len=L, nseq=nseq, precise_state=precise_state),
        grid=(nblk, nt),
        in_specs=[pl.BlockSpec((rows_tile, LANES), lambda q, i: (i, q)),
                  pl.BlockSpec((None, nseq, sl), blk),
                  pl.BlockSpec((None, kl, kl), blk),
                  pl.BlockSpec((None, nparts, kl, sl), lambda q, i: (q, 0, 0, 0)),
                  pl.BlockSpec((None, sl, kl), blk),
                  pl.BlockSpec((None, 1, sl), blk),
                  pl.BlockSpec((None, 1, sl), blk),
                  pl.BlockSpec((None, 1, LANES), blk),
                  pl.BlockSpec((None, LANES, LANES), blk),
                  pl.BlockSpec((None, 1, LANES), blk)],
        out_specs=[pl.BlockSpec((rows_tile, LANES), lambda q, i: (i, q)),
                   pl.BlockSpec((None, nseq, sl), blk)],
        out_shape=[jax.ShapeDtypeStruct((mrows, width), BF16),
                   jax.ShapeDtypeStruct((nblk, nseq, sl), F32)],
        scratch_shapes=[pltpu.VMEM((nseq, sl), F32),
                        pltpu.VMEM((chunks_per_tile * nseq, sl), F32),
                        pltpu.VMEM((rows_tile, LANES), F32)],
        compiler_params=_cparams("parallel", "arbitrary"),
        name="s5_mixer",
    )(u_rows, h0, tmat.astype(BF16), w_parts, vmat.astype(BF16), a1, a2, drow, gwb, gbrow)
    hout = hout.transpose(1, 0, 2).reshape(nseq, g_total, 2, n)
    return y, hout[:, :, 0, :], hout[:, :, 1, :]


def _lane_shift_down(x, s):
    lanes = x.shape[1]
    lane = lax.broadcasted_iota(jnp.int32, x.shape, 1)
    return jnp.where(lane < lanes - s, pltpu.roll(x, lanes - s, 1), 0.0)


def _lane_shift_up(x, s):
    lane = lax.broadcasted_iota(jnp.int32, x.shape, 1)
    return jnp.where(lane >= s, pltpu.roll(x, s, 1), 0.0)


def _fox_sample_kernel(pt_ref, q_ref, knt_ref, vnt_ref, lfnt_ref, *rest, pages_per_step, n_new):
    pp = pages_per_step
    k_refs = rest[:pp]
    v_refs = rest[pp:2 * pp]
    lf_refs = rest[2 * pp:3 * pp]
    o_ref, m_ref, l_ref, acc_ref, carry_ref, qbd_ref, ccol_ref = rest[3 * pp:]
    g = pl.program_id(1)
    nrow = n_new * N_HEADS
    width = q_ref.shape[1]
    page = lf_refs[0].shape[1]
    headmask = (lax.broadcasted_iota(jnp.int32, (N_HEADS, width), 0)
                == lax.broadcasted_iota(jnp.int32, (N_HEADS, width), 1) // HEAD_DIM)
    lane = lax.broadcasted_iota(jnp.int32, (N_HEADS, page), 1)

    def new_cumsum_t():
        c = lfnt_ref[...]
        s = 1
        while s < n_new:
            c = c + _lane_shift_up(c, s)
            s *= 2
        return c

    def tile_rows(x):
        return jnp.concatenate([x] * n_new, axis=0)

    @pl.when(g == 0)
    def _():
        m_ref[...] = jnp.full_like(m_ref, NEG_INF)
        l_ref[...] = jnp.zeros_like(l_ref)
        acc_ref[...] = jnp.zeros_like(acc_ref)
        carry_ref[...] = jnp.zeros_like(carry_ref)
        q = q_ref[...].astype(F32)
        cn = new_cumsum_t()
        for qi in range(n_new):
            qrow = jnp.broadcast_to(q[qi:qi + 1, :], (N_HEADS, width))
            qbd_ref[qi * N_HEADS:(qi + 1) * N_HEADS, :] = jnp.where(headmask, qrow, 0.0).astype(BF16)
            ccol_ref[qi * N_HEADS:(qi + 1) * N_HEADS, :] = jnp.sum(
                jnp.where(lane == qi, cn, 0.0), axis=1, keepdims=True)

    def update(s_list, vt_list):
        m_old = m_ref[...]
        mx = s_list[0]
        for s in s_list[1:]:
            mx = jnp.maximum(mx, s)
        m_new = jnp.maximum(m_old, jnp.max(mx, axis=1, keepdims=True))
        alpha = jnp.exp(m_old - m_new)
        psum = jnp.zeros(s_list[0].shape, F32)
        pv = jnp.zeros(acc_ref.shape, F32)
        for s, vt in zip(s_list, vt_list):
            p = jnp.exp(s - m_new)
            psum = psum + p
            pv = pv + _dot_nt(p.astype(BF16), vt)
        l_ref[...] = alpha * l_ref[...] + jnp.sum(psum, axis=1, keepdims=True)
        m_ref[...] = m_new
        acc_ref[...] = alpha * acc_ref[...] + pv

    qbd = qbd_ref[...]
    ccol = ccol_ref[...]
    carry = carry_ref[...]
    s_list, vt_list = [], []
    for pi in range(pp):
        kt = k_refs[pi][...]
        kt2 = kt.reshape(kt.shape[0] * kt.shape[1], kt.shape[2]).astype(BF16)
        lf = lf_refs[pi][...]
        rs = _lane_shift_down(lf, 1)
        s = 1
        while s < page:
            rs = rs + _lane_shift_down(rs, s)
            s *= 2
        rs = rs + carry
        carry = carry + jnp.sum(lf, axis=1, keepdims=True)
        s_list.append(_dot(qbd, kt2) + tile_rows(rs) + ccol)
        vt = v_refs[pi][...]
        vt_list.append(vt.reshape(vt.shape[0] * vt.shape[1], vt.shape[2]).astype(BF16))
    carry_ref[...] = carry
    update(s_list, vt_list)

    @pl.when(g == pl.num_programs(1) - 1)
    def _():
        s = _dot(qbd, knt_ref[...]) + ccol - tile_rows(new_cumsum_t())
        kj = lax.broadcasted_iota(jnp.int32, (nrow, page), 1)
        qi = lax.broadcasted_iota(jnp.int32, (nrow, page), 0) // N_HEADS
        s = jnp.where(kj <= qi, s, NEG_INF)
        update([s], [vnt_ref[...]])
        o = acc_ref[...] / l_ref[...]
        for q_i in range(n_new):
            blk = o[q_i * N_HEADS:(q_i + 1) * N_HEADS, :]
            o_ref[q_i:q_i + 1, :] = jnp.sum(jnp.where(headmask, blk, 0.0), axis=0,
                                            keepdims=True).astype(o_ref.dtype)


def fox_sample_attention(qb, kb_new, vb_new, logf_new, cache_k, cache_v, cache_logf, page_table,
                         *, pages_per_step):
    nb, n_new, a = qb.shape
    n_pool, page, nh, hd = cache_k.shape
    n_pages = page_table.shape[1]
    pp = pages_per_step
    assert n_new <= page and nh == N_HEADS
    ckt = cache_k.transpose(0, 2, 3, 1)
    cvt = cache_v.transpose(0, 2, 3, 1)
    clt = cache_logf.transpose(0, 2, 1)
    lane_pad = ((0, 0), (0, 0), (0, page - n_new))
    knt = jnp.pad(kb_new.transpose(0, 2, 1), lane_pad)
    vnt = jnp.pad(vb_new.transpose(0, 2, 1), lane_pad)
    lfnt = jnp.pad(logf_new.transpose(0, 2, 1), lane_pad)
    pt = page_table.reshape(-1).astype(jnp.int32)

    def page_map(pi, ndim):
        return lambda b, g, pt_ref: (pt_ref[b * n_pages + (n_pages - 1 - (g * pp + pi))],) + (0,) * (ndim - 1)

    per_b = lambda b, g, pt_ref: (b, 0, 0)
    nrow = n_new * nh
    grid_spec = pltpu.PrefetchScalarGridSpec(
        num_scalar_prefetch=1,
        grid=(nb, n_pages // pp),
        in_specs=[pl.BlockSpec((None, n_new, a), per_b),
                  pl.BlockSpec((None, a, page), per_b),
                  pl.BlockSpec((None, a, page), per_b),
                  pl.BlockSpec((None, nh, page), per_b)]
                 + [pl.BlockSpec((None, nh, hd, page), page_map(pi, 4)) for pi in range(pp)]
                 + [pl.BlockSpec((None, nh, hd, page), page_map(pi, 4)) for pi in range(pp)]
                 + [pl.BlockSpec((None, nh, page), page_map(pi, 3)) for pi in range(pp)],
        out_specs=pl.BlockSpec((None, n_new, a), per_b),
        scratch_shapes=[pltpu.VMEM((nrow, 1), F32), pltpu.VMEM((nrow, 1), F32),
                        pltpu.VMEM((nrow, a), F32), pltpu.VMEM((nh, 1), F32),
                        pltpu.VMEM((nrow, a), BF16), pltpu.VMEM((nrow, 1), F32)],
    )
    return pl.pallas_call(
        functools.partial(_fox_sample_kernel, pages_per_step=pp, n_new=n_new),
        grid_spec=grid_spec,
        out_shape=jax.ShapeDtypeStruct((nb, n_new, a), BF16),
        compiler_params=_cparams("parallel", "arbitrary"),
        name="fox_sample",
    )(pt, qb, knt, vnt, lfnt, *([ckt] * pp), *([cvt] * pp), *([clt] * pp))


def _layer_norm(x, g, b):
    mu = jnp.mean(x, axis=-1, keepdims=True)
    xc = x - mu
    var = jnp.mean(xc * xc, axis=-1, keepdims=True)
    return xc * lax.rsqrt(var + LN_EPS) * g + b


SUBLANES = 8


def _to_row_tiles(x, ref):
    rows = x.shape[0]
    for c in range(SUBLANES):
        ref[pl.ds(c, rows, stride=SUBLANES), :] = x[:, c * LANES:(c + 1) * LANES]


def _from_row_tiles(ref, first_row, rows):
    return jnp.concatenate(
        [ref[pl.ds(first_row * SUBLANES + c, rows, stride=SUBLANES), :] for c in range(SUBLANES)], axis=1)


def _outproj_router_kernel(*refs, n_experts, aliased, n_tiles):
    if aliased:
        refs = refs[2:]
    ins, (h_ref, ht_ref, idx_ref, w_ref, rank_ref, cnt_out_ref, cnt_scr) = refs[:11], refs[11:]
    cnt_in_ref = ins[10]
    i = pl.program_id(0)

    @pl.when(i == 0)
    def _():
        cnt_scr[...] = cnt_in_ref[...]

    @pl.when(i < n_tiles)
    def _():
        _outproj_router_tile(*ins[:10], h_ref, ht_ref, idx_ref, w_ref, rank_ref, cnt_scr, n_experts=n_experts)

    @pl.when(i >= n_tiles)
    def _():
        h_ref[...] = jnp.zeros_like(h_ref)
        ht_ref[...] = jnp.zeros_like(ht_ref)

    cnt_out_ref[...] = cnt_scr[...]


def _outproj_router_tile(att_ref, ssm_ref, x_ref, woa_ref, wos_ref, g_ref, b_ref, rw_ref, rb_ref,
                         tri_ref, h_ref, ht_ref, idx_ref, w_ref, rank_ref, cnt_scr, *, n_experts):
    tm = x_ref.shape[0]
    att = att_ref[...]
    ssm = ssm_ref[...].reshape(tm, ssm_ref.shape[-1])
    mixed = _dot(att, woa_ref[...]) + _dot(ssm, wos_ref[...])
    h = _layer_norm(DN_ALPHA * x_ref[...] + mixed, g_ref[...], b_ref[...])
    h_ref[...] = h
    _to_row_tiles(h, ht_ref)
    h_hi = h.astype(BF16)
    h_lo = (h - h_hi.astype(F32)).astype(BF16)
    logits = _dot(h_hi, rw_ref[0]) + _dot(h_lo, rw_ref[0]) + _dot(h_hi, rw_ref[1]) + rb_ref[...]
    lane = lax.broadcasted_iota(jnp.int32, logits.shape, 1)
    logits = jnp.where(lane < n_experts, logits, NEG_INF)
    vals, idxs = [], []
    for _ in range(TOP_K):
        mx = jnp.max(logits, axis=1, keepdims=True)
        ix = jnp.min(jnp.where(logits == mx, lane, LANES), axis=1, keepdims=True)
        vals.append(mx)
        idxs.append(ix)
        logits = jnp.where(lane == ix, NEG_INF, logits)
    es = [jnp.exp(v - vals[0]) for v in vals]
    denom = es[0] + es[1] + es[2] + es[3]
    lane8 = lax.broadcasted_iota(jnp.int32, (tm, idx_ref.shape[1]), 1)
    idx_out = jnp.zeros(lane8.shape, jnp.int32)
    w_out = jnp.zeros(lane8.shape, F32)
    onehot = jnp.zeros(logits.shape, F32)
    for k in range(TOP_K):
        onehot = onehot + (lane == idxs[k]).astype(F32)
    before = _dot(tri_ref[...], onehot.astype(BF16)) + cnt_scr[...]
    cnt_scr[...] = cnt_scr[...] + jnp.sum(onehot, axis=0, keepdims=True)
    rank_out = jnp.zeros(lane8.shape, jnp.int32)
    for k in range(TOP_K):
        idx_out = jnp.where(lane8 == k, idxs[k], idx_out)
        w_out = jnp.where(lane8 == k, es[k] / denom, w_out)
        rank_k = jnp.sum(jnp.where(lane == idxs[k], before, 0.0), axis=1, keepdims=True)
        rank_out = jnp.where(lane8 == k, rank_k.astype(jnp.int32), rank_out)
    idx_ref[...] = idx_out
    w_ref[...] = w_out
    rank_ref[...] = rank_out


ROUTE_COLS = 8


def outproj_router(att, ssm, x2, w_out, ln_g, ln_b, router_w, router_b, counts_in, *, tm, m_total,
                   row_block_offset, h_all, ssm_chunked):
    m, d = x2.shape
    a = att.shape[1]
    n_experts = router_w.shape[1]
    nt = m // tm
    woa = w_out[:a].astype(BF16)
    wos = w_out[a:].astype(BF16)
    rw = jnp.pad(router_w.astype(F32), ((0, 0), (0, LANES - n_experts)))
    rw_hi = rw.astype(BF16)
    rw_parts = jnp.stack([rw_hi, (rw - rw_hi.astype(F32)).astype(BF16)])
    rb = jnp.pad(router_b.astype(F32), (0, LANES - n_experts)).reshape(1, LANES)
    aliased = h_all is not None
    n_fill = 0 if aliased else m_total // tm - nt - row_block_offset
    clamp = lambda i: jnp.minimum(i, nt - 1)
    row = lambda i: (clamp(i), 0)
    full = lambda i: (0, 0)
    if ssm_chunked is None:
        ssm_spec = pl.BlockSpec((tm, ssm.shape[-1]), row)
    else:
        nseq, slen, L = ssm_chunked
        tps = slen // tm
        ssm_spec = pl.BlockSpec((tm // L, None, L, ssm.shape[-1]),
                                lambda i: (clamp(i) % tps, clamp(i) // tps, 0, 0))
    in_specs = [pl.BlockSpec((tm, a), row), ssm_spec, pl.BlockSpec((tm, d), row),
                pl.BlockSpec((a, d), full), pl.BlockSpec((w_out.shape[0] - a, d), full),
                pl.BlockSpec((1, d), full), pl.BlockSpec((1, d), full),
                pl.BlockSpec((2, d, LANES), lambda i: (0, 0, 0)), pl.BlockSpec((1, LANES), full),
                pl.BlockSpec((tm, tm), full), pl.BlockSpec((1, LANES), full)]
    tri_strict = _tri_ones(tm, -1)
    args = [att, ssm, x2, woa, wos, ln_g.astype(F32).reshape(1, d), ln_b.astype(F32).reshape(1, d), rw_parts, rb,
            tri_strict, counts_in]
    assert d == SUBLANES * LANES
    aliases = {}
    if aliased:
        in_specs = [pl.BlockSpec(memory_space=pl.ANY)] * 2 + in_specs
        args = list(h_all) + args
        aliases = {0: 0, 1: 1}
    return pl.pallas_call(
        functools.partial(_outproj_router_kernel, n_experts=n_experts, aliased=aliased, n_tiles=nt),
        grid=(nt + n_fill,),
        in_specs=in_specs,
        out_specs=[pl.BlockSpec((tm, d), lambda i: (i + row_block_offset, 0)),
                   pl.BlockSpec((tm * SUBLANES, LANES), lambda i: (i + row_block_offset, 0)),
                   pl.BlockSpec((tm, ROUTE_COLS), row), pl.BlockSpec((tm, ROUTE_COLS), row),
                   pl.BlockSpec((tm, ROUTE_COLS), row), pl.BlockSpec((1, LANES), full)],
        out_shape=[jax.ShapeDtypeStruct((m_total, d), F32),
                   jax.ShapeDtypeStruct((m_total * SUBLANES, LANES), F32),
                   jax.ShapeDtypeStruct((m, ROUTE_COLS), jnp.int32),
                   jax.ShapeDtypeStruct((m, ROUTE_COLS), F32),
                   jax.ShapeDtypeStruct((m, ROUTE_COLS), jnp.int32),
                   jax.ShapeDtypeStruct((1, LANES), F32)],
        input_output_aliases=aliases,
        scratch_shapes=[pltpu.VMEM((1, LANES), F32)],
        compiler_params=_cparams("arbitrary"),
        name="outproj_router",
    )(*args)


DMA_UNROLL = 8


def _for_each_row(n_rows, fn):
    def body(j, carry):
        for u in range(DMA_UNROLL):
            fn(j * DMA_UNROLL + u, u)
        return carry
    lax.fori_loop(0, n_rows // DMA_UNROLL, body, 0)


def _tile_rows(ref, row_tile):
    return ref.at[pl.ds(pl.multiple_of(row_tile * SUBLANES, SUBLANES), SUBLANES), :]


def _scatter_rows_kernel(base_ref, ztile_ref, idx_ref, rank_ref, ht_ref, zeros_ref, xs_ref, sem, zsem, *, tile):
    n_assign = idx_ref.shape[1]
    n_experts = base_ref.shape[0]

    @pl.when(pl.program_id(0) == 0)
    def _():
        def zero_copy(e):
            row0 = pl.multiple_of(ztile_ref[e] * SUBLANES, tile * SUBLANES)
            return pltpu.make_async_copy(zeros_ref, xs_ref.at[pl.ds(row0, tile * SUBLANES), :], zsem)
        for e in range(n_experts):
            zero_copy(e).start()
        for e in range(n_experts):
            zero_copy(e).wait()

    def row_copy(a, dst_row):
        tok = lax.shift_right_logical(a, 2)
        return pltpu.make_async_copy(_tile_rows(ht_ref, tok), _tile_rows(xs_ref, dst_row), sem)

    def start(a, u):
        row_copy(a, base_ref[idx_ref[0, a]] + rank_ref[0, a]).start(priority=u % 2)

    def wait(a, u):
        row_copy(a, 0).wait()

    _for_each_row(n_assign, start)
    _for_each_row(n_assign, wait)


def scatter_rows(h_tiles, idx_flat, rank_flat, base, ztile, *, n_rows_out, tokens_per_tile, expert_tile):
    assert TOP_K == 4
    m = h_tiles.shape[0] // SUBLANES
    nt = m // tokens_per_tile
    na = tokens_per_tile * TOP_K
    grid_spec = pltpu.PrefetchScalarGridSpec(
        num_scalar_prefetch=2,
        grid=(nt,),
        in_specs=[pl.BlockSpec((None, 1, na), lambda i, b, z: (i, 0, 0), memory_space=pltpu.SMEM),
                  pl.BlockSpec((None, 1, na), lambda i, b, z: (i, 0, 0), memory_space=pltpu.SMEM),
                  pl.BlockSpec((tokens_per_tile * SUBLANES, LANES), lambda i, b, z: (i, 0)),
                  pl.BlockSpec((expert_tile * SUBLANES, LANES), lambda i, b, z: (0, 0))],
        out_specs=pl.BlockSpec(memory_space=pl.ANY),
        scratch_shapes=[pltpu.SemaphoreType.DMA, pltpu.SemaphoreType.DMA],
    )
    return pl.pallas_call(
        functools.partial(_scatter_rows_kernel, tile=expert_tile),
        grid_spec=grid_spec,
        out_shape=jax.ShapeDtypeStruct((n_rows_out * SUBLANES, LANES), h_tiles.dtype),
        compiler_params=_cparams("arbitrary"),
        name="moe_scatter",
    )(base, ztile, idx_flat.reshape(nt, 1, na), rank_flat.reshape(nt, 1, na), h_tiles,
      jnp.zeros((expert_tile * SUBLANES, LANES), h_tiles.dtype))


def _expert_kernel(te_ref, tv_ref, x_ref, wup_ref, bup_ref, wdn_ref, bdn_ref, y_ref, wup_bf, wdn_bf):
    i = pl.program_id(0)
    de = wdn_ref.shape[0]
    tm = x_ref.shape[0] // SUBLANES

    @pl.when((i == 0) | (te_ref[i] != te_ref[jnp.maximum(i - 1, 0)]))
    def _():
        wup_bf[...] = wup_ref[...].astype(BF16)
        wdn_bf[...] = wdn_ref[...].astype(BF16)

    @pl.when(tv_ref[i] > 0)
    def _():
        xb = _from_row_tiles(x_ref, 0, tm).astype(BF16)
        hcat = _dot(xb, wup_bf[...]) + bup_ref[...]
        x_glu = jnp.minimum(hcat[:, :de], SWIGLU_LIMIT)
        x_lin = jnp.clip(hcat[:, de:], -SWIGLU_LIMIT, SWIGLU_LIMIT)
        act = x_glu * jax.nn.sigmoid(SWIGLU_ALPHA * x_glu) * (x_lin + 1.0)
        _to_row_tiles(_dot(act.astype(BF16), wdn_bf[...]) + bdn_ref[...], y_ref)

    @pl.when(tv_ref[i] == 0)
    def _():
        y_ref[...] = jnp.zeros_like(y_ref)


def expert_mlp(xs, tile_expert, tile_valid, w_up, b_up, w_down, b_down, *, tm):
    npad = xs.shape[0] // SUBLANES
    n_e, d, d2 = w_up.shape
    de = w_down.shape[1]
    dout = w_down.shape[2]
    assert d == SUBLANES * LANES and dout == SUBLANES * LANES
    nt = npad // tm
    grid_spec = pltpu.PrefetchScalarGridSpec(
        num_scalar_prefetch=2,
        grid=(nt,),
        in_specs=[pl.BlockSpec((tm * SUBLANES, LANES), lambda i, te, tv: (jnp.where(tv[i] > 0, i, te[nt]), 0)),
                  pl.BlockSpec((None, d, d2), lambda i, te, tv: (te[i], 0, 0)),
                  pl.BlockSpec((None, 1, d2), lambda i, te, tv: (te[i], 0, 0)),
                  pl.BlockSpec((None, de, dout), lambda i, te, tv: (te[i], 0, 0)),
                  pl.BlockSpec((None, 1, dout), lambda i, te, tv: (te[i], 0, 0))],
        out_specs=pl.BlockSpec((tm * SUBLANES, LANES), lambda i, te, tv: (i, 0)),
        scratch_shapes=[pltpu.VMEM((d, d2), BF16), pltpu.VMEM((de, dout), BF16)],
    )
    return pl.pallas_call(
        _expert_kernel,
        grid_spec=grid_spec,
        out_shape=jax.ShapeDtypeStruct((npad * SUBLANES, LANES), F32),
        compiler_params=_cparams("arbitrary"),
        name="moe_experts",
    )(tile_expert, tile_valid, xs, w_up.astype(F32),
      b_up.astype(F32).reshape(n_e, 1, d2), w_down.astype(F32), b_down.astype(F32).reshape(n_e, 1, dout))


def _combine_ln_kernel(base_ref, idx_ref, rank_ref, idxn_ref, rankn_ref, h_ref, w_ref, g_ref, b_ref, ys_ref,
                       o_ref, buf0, buf1, sem0, sem1):
    tm = h_ref.shape[0]
    n_assign = idx_ref.shape[1]
    i = pl.program_id(0)
    last = pl.num_programs(0) - 1

    def gather(idx_r, rank_r, buf, sem):
        def row_copy(a, src_row):
            dst = lax.bitwise_and(a, TOP_K - 1) * tm + lax.shift_right_logical(a, 2)
            return pltpu.make_async_copy(_tile_rows(ys_ref, src_row), _tile_rows(buf, dst), sem)

        def start(a, u):
            row_copy(a, base_ref[idx_r[0, a]] + rank_r[0, a]).start(priority=u % 2)

        def wait(a, u):
            row_copy(a, 0).wait()

        return start, wait

    def step(buf_cur, sem_cur, buf_nxt, sem_nxt):
        start_cur, wait_cur = gather(idx_ref, rank_ref, buf_cur, sem_cur)
        start_nxt, _ = gather(idxn_ref, rankn_ref, buf_nxt, sem_nxt)

        @pl.when(i == 0)
        def _():
            _for_each_row(n_assign, start_cur)

        @pl.when(i < last)
        def _():
            _for_each_row(n_assign, start_nxt)

        _for_each_row(n_assign, wait_cur)
        w = w_ref[...]
        moe = w[:, 0:1] * _from_row_tiles(buf_cur, 0, tm)
        for k in range(1, TOP_K):
            moe = moe + w[:, k:k + 1] * _from_row_tiles(buf_cur, k * tm, tm)
        o_ref[...] = _layer_norm(DN_ALPHA * h_ref[...] + moe, g_ref[...], b_ref[...])

    @pl.when(i % 2 == 0)
    def _():
        step(buf0, sem0, buf1, sem1)

    @pl.when(i % 2 == 1)
    def _():
        step(buf1, sem1, buf0, sem0)


def combine_ln(h_all, ys, idx_flat, rank_flat, top_w, base, ln_g, ln_b, *, tm, m, row_block_offset):
    assert TOP_K == 4
    d = h_all.shape[1]
    nt = m // tm
    na = tm * TOP_K
    grid_spec = pltpu.PrefetchScalarGridSpec(
        num_scalar_prefetch=1,
        grid=(nt,),
        in_specs=[pl.BlockSpec((None, 1, na), lambda i, b: (i, 0, 0), memory_space=pltpu.SMEM),
                  pl.BlockSpec((None, 1, na), lambda i, b: (i, 0, 0), memory_space=pltpu.SMEM),
                  pl.BlockSpec((None, 1, na), lambda i, b: (jnp.minimum(i + 1, nt - 1), 0, 0), memory_space=pltpu.SMEM),
                  pl.BlockSpec((None, 1, na), lambda i, b: (jnp.minimum(i + 1, nt - 1), 0, 0), memory_space=pltpu.SMEM),
                  pl.BlockSpec((tm, d), lambda i, b: (i + row_block_offset, 0)),
                  pl.BlockSpec((tm, ROUTE_COLS), lambda i, b: (i, 0)),
                  pl.BlockSpec((1, d), lambda i, b: (0, 0)),
                  pl.BlockSpec((1, d), lambda i, b: (0, 0)),
                  pl.BlockSpec(memory_space=pl.ANY)],
        out_specs=pl.BlockSpec((tm, d), lambda i, b: (i, 0)),
        scratch_shapes=[pltpu.VMEM((TOP_K * tm * SUBLANES, LANES), F32),
                        pltpu.VMEM((TOP_K * tm * SUBLANES, LANES), F32),
                        pltpu.SemaphoreType.DMA, pltpu.SemaphoreType.DMA],
    )
    idx3 = idx_flat.reshape(nt, 1, na)
    rank3 = rank_flat.reshape(nt, 1, na)
    return pl.pallas_call(
        _combine_ln_kernel,
        grid_spec=grid_spec,
        out_shape=jax.ShapeDtypeStruct((m, d), F32),
        compiler_params=_cparams("arbitrary"),
        name="moe_combine_ln",
    )(base, idx3, rank3, idx3, rank3, h_all, top_w,
      ln_g.astype(F32).reshape(1, d), ln_b.astype(F32).reshape(1, d), ys)


def _tile_plan(counts, n_assign, tm):
    n_experts = counts.shape[0]
    n_tiles = n_assign // tm + n_experts
    ntile_e = (counts + tm - 1) // tm
    tile_end = jnp.cumsum(ntile_e)
    tile_start = tile_end - ntile_e
    tile_ids = jnp.arange(n_tiles, dtype=jnp.int32)
    tile_expert = jnp.minimum((tile_ids[:, None] >= tile_end[None, :]).sum(axis=1), n_experts - 1)
    n_used = tile_end[-1]
    tile_valid = (tile_ids < n_used).astype(jnp.int32)
    tile_expert = jnp.concatenate([tile_expert, jnp.maximum(n_used - 1, 0)[None]]).astype(jnp.int32)
    base = (tile_start * tm).astype(jnp.int32)
    ztile = jnp.where(ntile_e > 0, (tile_end - 1) * tm, (n_tiles - 1) * tm).astype(jnp.int32)
    return n_tiles, tile_expert, tile_valid, base, ztile


S5_CHUNK = 16
TOKEN_TILE = 512
ATTN_TILE = 1024
ATTN_KEY_TILE = 1024
S5_CHUNKS_PER_TILE = 16
PAGES_PER_STEP = 32
EXPERT_TILE = 512
SCATTER_TILE = 512
COMBINE_TILE = 256


def kernel(x_prompt, x_sample, cache_k, cache_v, cache_logf, state_ssm_re, state_ssm_im, page_table, w_in, b_forget, ssm_a_re, ssm_a_im, ssm_log_dt, ssm_b_re, ssm_b_im, ssm_c_re, ssm_c_im, ssm_d, ssm_glu_w, ssm_glu_b, w_out, ln1_g, ln1_b, router_w, router_b, w_up, b_up, w_down, b_down, ln2_g, ln2_b):
    bsz, slen, d = x_prompt.shape
    dec_b, dec_t, _ = x_sample.shape
    m_p = bsz * slen
    m_s = dec_b * dec_t
    m_all = m_p + m_s
    n_groups, n_state = ssm_a_re.shape
    n_experts = router_w.shape[1]
    tm = TOKEN_TILE
    assert m_s == tm and slen % tm == 0 and m_p % tm == 0
    assert (m_all * TOP_K) % EXPERT_TILE == 0 and m_p % COMBINE_TILE == 0 and m_s % COMBINE_TILE == 0

    qb, k_p, v_p, kb, vb, logf_p, u_p = in_projection(x_prompt, w_in, b_forget, tm=tm, chunk_len=S5_CHUNK,
                                                      aug=True)
    att_p = fox_prompt_attention(qb, kb, vb, bsz, slen, tq=ATTN_TILE, tk=ATTN_KEY_TILE)
    tables_p = _s5_tables(ssm_a_re, ssm_a_im, ssm_log_dt, ssm_b_re, ssm_b_im, ssm_c_re, ssm_c_im, S5_CHUNK)
    zeros_state = jnp.zeros((bsz, n_groups, n_state), F32)
    ssm_p, hre_p, him_p = s5_mixer(u_p.reshape(m_p, u_p.shape[-1]), zeros_state, zeros_state, tables_p,
                                   ssm_d, ssm_glu_w, ssm_glu_b, nseq=bsz, chunk_len=S5_CHUNK,
                                   chunks_per_tile=S5_CHUNKS_PER_TILE, precise_state=False)

    qs, k_s, v_s, ksb, vsb, logf_s, u_s = in_projection(x_sample.reshape(1, m_s, d), w_in, b_forget,
                                                         tm=tm, chunk_len=None, aug=False)
    a = ATTN_WIDTH
    att_s = fox_sample_attention(qs.reshape(dec_b, dec_t, a), ksb.reshape(dec_b, dec_t, a),
                                 vsb.reshape(dec_b, dec_t, a), logf_s.reshape(dec_b, dec_t, N_HEADS),
                                 cache_k, cache_v, cache_logf, page_table, pages_per_step=PAGES_PER_STEP)
    tables_s = _s5_tables(ssm_a_re, ssm_a_im, ssm_log_dt, ssm_b_re, ssm_b_im, ssm_c_re, ssm_c_im, dec_t)
    ssm_s, hre_s, him_s = s5_mixer(u_s, state_ssm_re, state_ssm_im, tables_s, ssm_d, ssm_glu_w, ssm_glu_b,
                                   nseq=dec_b, chunk_len=dec_t, chunks_per_tile=1, precise_state=True)

    ssm_p4 = ssm_p.reshape(slen // S5_CHUNK, bsz, S5_CHUNK, ssm_p.shape[-1])
    zero_counts = jnp.zeros((1, LANES), F32)
    h_all, h_tiles, idx_p, w_p, rank_p, cnt_p = outproj_router(
        att_p, ssm_p4, x_prompt.reshape(m_p, d), w_out, ln1_g, ln1_b, router_w, router_b, zero_counts,
        tm=tm, m_total=m_all, row_block_offset=0, h_all=None, ssm_chunked=(bsz, slen, S5_CHUNK))
    h_all, h_tiles, idx_s, w_s, rank_s, cnt_all = outproj_router(
        att_s.reshape(m_s, a), ssm_s, x_sample.reshape(m_s, d), w_out, ln1_g, ln1_b, router_w, router_b, cnt_p,
        tm=tm, m_total=m_all, row_block_offset=m_p // tm, h_all=(h_all, h_tiles), ssm_chunked=None)

    counts = cnt_all[0, :n_experts].astype(jnp.int32)
    n_tiles, tile_expert, tile_valid, base, ztile = _tile_plan(counts, m_all * TOP_K, EXPERT_TILE)
    flat = lambda t: t[:, :TOP_K].reshape(-1)
    idx_pf, idx_sf, rank_pf, rank_sf = flat(idx_p), flat(idx_s), flat(rank_p), flat(rank_s)
    xs = scatter_rows(h_tiles, jnp.concatenate([idx_pf, idx_sf]), jnp.concatenate([rank_pf, rank_sf]), base, ztile,
                      n_rows_out=n_tiles * EXPERT_TILE, tokens_per_tile=SCATTER_TILE, expert_tile=EXPERT_TILE)
    ys = expert_mlp(xs, tile_expert, tile_valid, w_up, b_up, w_down, b_down, tm=EXPERT_TILE)
    y_p = combine_ln(h_all, ys, idx_pf, rank_pf, w_p, base, ln2_g, ln2_b, tm=COMBINE_TILE, m=m_p,
                     row_block_offset=0)
    y_s = combine_ln(h_all, ys, idx_sf, rank_sf, w_s, base, ln2_g, ln2_b, tm=COMBINE_TILE, m=m_s,
                     row_block_offset=m_p // COMBINE_TILE)

    hd = (N_HEADS, HEAD_DIM)
    return (y_p.reshape(bsz, slen, d), y_s.reshape(dec_b, dec_t, d),
            k_p.reshape(bsz, *hd, slen).transpose(0, 3, 1, 2), v_p.reshape(bsz, *hd, slen).transpose(0, 3, 1, 2),
            logf_p.reshape(bsz, slen, N_HEADS).astype(cache_logf.dtype),
            hre_p.astype(state_ssm_re.dtype), him_p.astype(state_ssm_im.dtype),
            k_s.reshape(dec_b, dec_t, *hd), v_s.reshape(dec_b, dec_t, *hd),
            logf_s.reshape(dec_b, dec_t, N_HEADS).astype(cache_logf.dtype),
            hre_s.astype(state_ssm_re.dtype), him_s.astype(state_ssm_im.dtype))
```

```python
import functools
import math

import jax
import jax.numpy as jnp
import numpy as np
from jax import lax
from jax.experimental import pallas as pl
from jax.experimental.pallas import tpu as pltpu

N_HEADS = 8
HEAD_DIM = 64
ATTN_WIDTH = N_HEADS * HEAD_DIM
SSM_GROUP = 16
SSM_STATE = 64
TOP_K = 4
SWIGLU_LIMIT = 7.0
SWIGLU_ALPHA = 1.702
LN_EPS = 1e-5
DEPTH = 1
DN_ALPHA = (2.0 * DEPTH) ** 0.25
ATTN_SCALE = HEAD_DIM ** -0.5

LANES = 128
GROUPS_PER_LANE_BLOCK = LANES // SSM_GROUP
STATE_LANES = GROUPS_PER_LANE_BLOCK * 2 * SSM_STATE
VMEM_LIMIT_BYTES = 56 * 1024 * 1024

BF16 = jnp.bfloat16
F32 = jnp.float32
NEG_INF = float("-inf")


def _cparams(*sem):
    return pltpu.CompilerParams(dimension_semantics=sem, vmem_limit_bytes=VMEM_LIMIT_BYTES)


def _split3(x):
    hi = x.astype(BF16)
    r1 = x - hi.astype(F32)
    mid = r1.astype(BF16)
    lo = (r1 - mid.astype(F32)).astype(BF16)
    return hi, mid, lo


def _dot(a, b):
    return jnp.dot(a, b, preferred_element_type=F32)


def _dot_nt(a, b):
    return lax.dot_general(a, b, (((1,), (1,)), ((), ())), preferred_element_type=F32)


LOG2E = math.log2(math.e)
AUG_C_LANE = HEAD_DIM
AUG_ONE_LANE = HEAD_DIM + 3


def _aug_tables():
    e = np.zeros((LANES, 2 * N_HEADS * LANES), np.float32)
    ones = np.zeros((1, 2 * N_HEADS * LANES), np.float32)
    koff = N_HEADS * LANES
    for h in range(N_HEADS):
        for p in range(3):
            src = p * N_HEADS + h
            e[src, h * LANES + AUG_C_LANE + p] = 1.0
            e[src, koff + h * LANES + AUG_ONE_LANE + p] = -1.0
            ones[0, h * LANES + AUG_ONE_LANE + p] = 1.0
            ones[0, koff + h * LANES + AUG_C_LANE + p] = 1.0
    return jnp.asarray(e, BF16), jnp.asarray(ones)


def _tri_ones(n, k):
    return jnp.asarray(np.tril(np.ones((n, n), np.float32), k), BF16)


def _spread_heads(x):
    low = lax.broadcasted_iota(jnp.int32, (1, LANES), 1) < HEAD_DIM
    blocks = []
    for j in range(x.shape[1] // LANES):
        pair = x[:, j * LANES:(j + 1) * LANES]
        blocks.append(jnp.where(low, pair, 0.0))
        blocks.append(jnp.where(low, pltpu.roll(pair, HEAD_DIM, 1), 0.0))
    return jnp.concatenate(blocks, axis=1)


def _inproj_kernel(x_ref, wqkv_ref, wf_ref, wu_ref, bf_ref, *rest, chunk_len, aug):
    if aug:
        tri_ref, e_ref, ones_ref, q_ref, k_ref, v_ref, kb_ref, vb_ref, logf_ref, u_ref, carry_ref = rest
    else:
        q_ref, k_ref, v_ref, kb_ref, vb_ref, logf_ref, u_ref = rest
    xb = x_ref[...].astype(BF16)
    qkv = _dot(xb, wqkv_ref[...])
    a = ATTN_WIDTH
    q = qkv[:, :a]
    k = qkv[:, a:2 * a]
    v = qkv[:, 2 * a:]
    if aug:
        k_ref[...] = k.T
        v_ref[...] = v.T
    else:
        k_ref[...] = k
        v_ref[...] = v
    vb_ref[...] = v.astype(BF16)
    f = _dot(xb, wf_ref[...]) + bf_ref[...]
    logf = jnp.minimum(f, 0.0) - jnp.log1p(jnp.exp(-jnp.abs(f)))
    logf_ref[...] = logf[:, :N_HEADS]

    if aug:
        @pl.when(pl.program_id(1) == 0)
        def _():
            carry_ref[...] = jnp.zeros_like(carry_ref)
        hi, mid, lo = _split3(logf)
        tri = tri_ref[...]
        c = _dot(tri, hi) + _dot(tri, mid) + _dot(tri, lo) + carry_ref[...]
        carry_ref[...] = c[-1:, :]
        c_hi, c_mid, c_lo = _split3(c * LOG2E)
        lane = lax.broadcasted_iota(jnp.int32, (1, LANES), 1)
        pieces = jnp.where(lane < N_HEADS, c_hi.astype(F32),
                           jnp.where(lane < 2 * N_HEADS, pltpu.roll(c_mid.astype(F32), N_HEADS, 1),
                                     jnp.where(lane < 3 * N_HEADS, pltpu.roll(c_lo.astype(F32), 2 * N_HEADS, 1),
                                               0.0)))
        extra = _dot(pieces.astype(BF16), e_ref[...]) + ones_ref[...]
        half = N_HEADS * LANES
        q_ref[...] = (_spread_heads(q) + extra[:, :half]).astype(BF16)
        kb_ref[...] = (_spread_heads(k) + extra[:, half:]).astype(BF16)
    else:
        q_ref[...] = q.astype(BF16)
        kb_ref[...] = k.astype(BF16)

    u = _dot(xb, wu_ref[...])
    if chunk_len is None:
        u_ref[...] = u
    else:
        u_ref[...] = u.reshape(u.shape[0] // chunk_len, chunk_len, u.shape[1])


def in_projection(x, w_in, b_forget, *, tm, chunk_len, aug):
    nseq, slen, d = x.shape
    m = nseq * slen
    nt = slen // tm
    a = ATTN_WIDTH
    x2 = x.reshape(m, d)
    q_scale = ATTN_SCALE * LOG2E if aug else ATTN_SCALE
    col_scale = jnp.concatenate([jnp.full((a,), q_scale, F32), jnp.ones((2 * a,), F32)])
    wqkv = (w_in[:, :3 * a] * col_scale).astype(BF16)
    wf = jnp.pad(w_in[:, 3 * a:3 * a + N_HEADS], ((0, 0), (0, LANES - N_HEADS))).astype(BF16)
    wu = w_in[:, 3 * a + N_HEADS:].astype(BF16)
    ssm_w = wu.shape[1]
    bf = jnp.pad(b_forget.astype(F32), (0, LANES - N_HEADS)).reshape(1, LANES)

    row = lambda b, i: (b * nt + i, 0)
    full = lambda b, i: (0, 0)
    tok = lambda w, dt: jax.ShapeDtypeStruct((m, w), dt)
    if chunk_len is None:
        u_shape = tok(ssm_w, F32)
        u_spec = pl.BlockSpec((tm, ssm_w), row)
    else:
        u_shape = jax.ShapeDtypeStruct((slen // chunk_len, nseq, chunk_len, ssm_w), F32)
        u_spec = pl.BlockSpec((tm // chunk_len, None, chunk_len, ssm_w), lambda b, i: (i, b, 0, 0))
    in_specs = [pl.BlockSpec((tm, d), row),
                pl.BlockSpec((d, 3 * a), full),
                pl.BlockSpec((d, LANES), full),
                pl.BlockSpec((d, ssm_w), full),
                pl.BlockSpec((1, LANES), full)]
    args = [x2, wqkv, wf, wu, bf]
    qk_w = a
    scratch = []
    kv_spec = pl.BlockSpec((tm, a), row)
    kv_shape = tok(a, F32)
    if aug:
        kv_spec = pl.BlockSpec((None, a, tm), lambda b, i: (b, 0, i))
        kv_shape = jax.ShapeDtypeStruct((nseq, a, slen), F32)
        e_mat, ones_row = _aug_tables()
        qk_w = N_HEADS * LANES
        in_specs += [pl.BlockSpec((tm, tm), full), pl.BlockSpec((LANES, 2 * qk_w), full),
                     pl.BlockSpec((1, 2 * qk_w), full)]
        args += [_tri_ones(tm, 0), e_mat, ones_row]
        scratch = [pltpu.VMEM((1, LANES), F32)]
    return pl.pallas_call(
        functools.partial(_inproj_kernel, chunk_len=chunk_len, aug=aug),
        grid=(nseq, nt),
        in_specs=in_specs,
        out_specs=[pl.BlockSpec((tm, qk_w), row), kv_spec, kv_spec,
                   pl.BlockSpec((tm, qk_w), row), pl.BlockSpec((tm, a), row),
                   pl.BlockSpec((tm, N_HEADS), row), u_spec],
        out_shape=[tok(qk_w, BF16), kv_shape, kv_shape, tok(qk_w, BF16), tok(a, BF16),
                   tok(N_HEADS, F32), u_shape],
        scratch_shapes=scratch,
        compiler_params=_cparams("parallel", "arbitrary"),
        name="in_projection",
    )(*args)


def _fox_prompt_kernel(q_ref, k_ref, v_ref, o_ref, m_ref, l_ref, acc_ref):
    i = pl.program_id(1)
    kk = pl.program_id(2)
    nk = pl.num_programs(2)
    tq = q_ref.shape[0]
    tk = k_ref.shape[0]

    @pl.when(kk == 0)
    def _():
        m_ref[...] = jnp.full_like(m_ref, NEG_INF)
        l_ref[...] = jnp.zeros_like(l_ref)
        acc_ref[...] = jnp.zeros_like(acc_ref)

    def step(masked):
        lane = lax.broadcasted_iota(jnp.int32, (1, LANES), 1)
        low = lane < HEAD_DIM
        if masked:
            qpos = lax.broadcasted_iota(jnp.int32, (tq, tk), 0)
            kpos = lax.broadcasted_iota(jnp.int32, (tq, tk), 1)
            causal = kpos <= qpos
        for blk in range(N_HEADS // 2):
            sl = slice(blk * LANES, (blk + 1) * LANES)
            v = v_ref[:, sl]
            pvs, alphas = [], []
            for half in range(2):
                h = 2 * blk + half
                hs = slice(h * LANES, (h + 1) * LANES)
                s = _dot_nt(q_ref[:, hs], k_ref[:, hs])
                if masked:
                    s = jnp.where(causal, s, NEG_INF)
                m_old = m_ref[:, h:h + 1]
                m_new = jnp.maximum(m_old, jnp.max(s, axis=1, keepdims=True))
                alpha = jnp.exp2(m_old - m_new)
                p = jnp.exp2(s - m_new)
                l_ref[:, h:h + 1] = alpha * l_ref[:, h:h + 1] + jnp.sum(p, axis=1, keepdims=True)
                m_ref[:, h:h + 1] = m_new
                pvs.append(_dot(p.astype(BF16), v))
                alphas.append(alpha)
            alpha2 = jnp.where(low, alphas[0], alphas[1])
            pv2 = jnp.where(low, pvs[0], pvs[1])
            acc_ref[:, sl] = alpha2 * acc_ref[:, sl] + pv2

    @pl.when(kk < i)
    def _():
        step(False)

    @pl.when(kk == i)
    def _():
        step(True)

    @pl.when(kk == nk - 1)
    def _():
        lane = lax.broadcasted_iota(jnp.int32, (1, LANES), 1)
        low = lane < HEAD_DIM
        for blk in range(N_HEADS // 2):
            sl = slice(blk * LANES, (blk + 1) * LANES)
            l2 = jnp.where(low, l_ref[:, 2 * blk:2 * blk + 1], l_ref[:, 2 * blk + 1:2 * blk + 2])
            o_ref[:, sl] = (acc_ref[:, sl] / l2).astype(o_ref.dtype)


def fox_prompt_attention(q_aug, k_aug, vb, nseq, slen, *, tq):
    m = nseq * slen
    nq = slen // tq
    a = ATTN_WIDTH
    aw = q_aug.shape[1]
    qmap = lambda b, i, kk: (b * nq + i, 0)
    kmap = lambda b, i, kk: (b * nq + jnp.minimum(kk, i), 0)
    return pl.pallas_call(
        _fox_prompt_kernel,
        grid=(nseq, nq, nq),
        in_specs=[pl.BlockSpec((tq, aw), qmap),
                  pl.BlockSpec((tq, aw), kmap),
                  pl.BlockSpec((tq, a), kmap)],
        out_specs=pl.BlockSpec((tq, a), qmap),
        out_shape=jax.ShapeDtypeStruct((m, a), BF16),
        scratch_shapes=[pltpu.VMEM((tq, N_HEADS), F32), pltpu.VMEM((tq, N_HEADS), F32),
                        pltpu.VMEM((tq, a), F32)],
        compiler_params=_cparams("parallel", "parallel", "arbitrary"),
        name="fox_prompt",
    )(q_aug, k_aug, vb)


def _s5_tables(a_re, a_im, log_dt, b_re, b_im, c_re, c_im, chunk_len):
    g, n = a_re.shape
    cdim = b_re.shape[2]
    nblk = g // GROUPS_PER_LANE_BLOCK
    gb = GROUPS_PER_LANE_BLOCK
    a_re = a_re.astype(F32)
    a_im = a_im.astype(F32)
    b_re = b_re.astype(F32)
    b_im = b_im.astype(F32)
    c_re = c_re.astype(F32)
    c_im = c_im.astype(F32)
    dt = jnp.exp(log_dt.astype(F32))[:, None]
    mag = jnp.exp(a_re * dt)
    ar = mag * jnp.cos(a_im * dt)
    ai = mag * jnp.sin(a_im * dt)
    den = a_re * a_re + a_im * a_im
    cr = ((ar - 1.0) * a_re + ai * a_im) / den
    ci = (ai * a_re - (ar - 1.0) * a_im) / den
    bbr = cr[..., None] * b_re - ci[..., None] * b_im
    bbi = cr[..., None] * b_im + ci[..., None] * b_re
    pr = [jnp.ones_like(ar)]
    pi = [jnp.zeros_like(ai)]
    for _ in range(chunk_len):
        pr_n = pr[-1] * ar - pi[-1] * ai
        pi_n = pr[-1] * ai + pi[-1] * ar
        pr.append(pr_n)
        pi.append(pi_n)
    pr = jnp.stack(pr)
    pi = jnp.stack(pi)
    L = chunk_len
    pbr = pr[:L, :, :, None] * bbr[None] - pi[:L, :, :, None] * bbi[None]
    pbi = pr[:L, :, :, None] * bbi[None] + pi[:L, :, :, None] * bbr[None]
    kmat = jnp.einsum('gon,lgni->lgoi', c_re, pbr) - jnp.einsum('gon,lgni->lgoi', c_im, pbi)
    eye = jnp.asarray(np.eye(gb, dtype=np.float32))
    bd = jnp.einsum('lqgoi,gh->lqgiho', kmat.reshape(L, nblk, gb, cdim, cdim), eye)
    bd = bd.reshape(L, nblk, gb * cdim, gb * cdim)
    steps = np.arange(L)
    shift = jnp.asarray((steps[None, None, :] - steps[None, :, None] == steps[:, None, None])
                        .astype(np.float32))
    tmat = jnp.einsum('lst,lqrc->qsrtc', shift, bd, precision=lax.Precision.HIGHEST)
    tmat = tmat.reshape(nblk, L * gb * cdim, L * gb * cdim)
    wr = pbr[::-1].reshape(L, nblk, gb, n, cdim)
    wi = pbi[::-1].reshape(L, nblk, gb, n, cdim)
    wri = jnp.stack([wr, wi], axis=3)
    w2 = wri.transpose(0, 1, 2, 5, 3, 4).reshape(L, nblk, gb, cdim, 2 * n)
    wmat = jnp.einsum('sqgix,gh->qsgihx', w2, eye).reshape(nblk, L * gb * cdim, gb * 2 * n)
    cpr = c_re[None] * pr[1:, :, None, :] - c_im[None] * pi[1:, :, None, :]
    cpi = c_re[None] * pi[1:, :, None, :] + c_im[None] * pr[1:, :, None, :]
    vri = jnp.stack([cpr, -cpi], axis=1).reshape(L, 2, nblk, gb, cdim, n)
    vs = vri.transpose(2, 3, 1, 5, 0, 4).reshape(nblk, gb, 2 * n, L, cdim)
    place = np.einsum('gh,op->gohp', np.eye(gb, dtype=np.float32), np.eye(cdim, dtype=np.float32))
    place = jnp.asarray(place.reshape(gb, cdim, gb * cdim))
    vmat = jnp.einsum('qgxto,gop->qgxtp', vs, place, precision=lax.Precision.HIGHEST)
    vmat = vmat.reshape(nblk, gb * 2 * n, L * gb * cdim)
    al_r = pr[L].reshape(nblk, gb, n)
    al_i = pi[L].reshape(nblk, gb, n)
    a1 = jnp.stack([al_r, al_r], axis=2).reshape(nblk, 1, gb * 2 * n)
    a2 = jnp.stack([-al_i, al_i], axis=2).reshape(nblk, 1, gb * 2 * n)
    return tmat, wmat, vmat, a1, a2


def _gelu_exact(y):
    return 0.5 * y * (1.0 + lax.erf(y * (2.0 ** -0.5)))


def _s5_kernel(u_ref, h0_ref, t_ref, w_ref, v_ref, a1_ref, a2_ref, d_ref, gw_ref, gb_ref,
               y_ref, hout_ref, h_scr, hp_scr, y_scr, *, chunk_len, nseq, precise_state):
    i = pl.program_id(1)
    L = chunk_len
    rows = u_ref.shape[0]
    n = rows // L
    nj = n // nseq

    @pl.when(i == 0)
    def _():
        h_scr[...] = h0_ref[...]

    xs = [u_ref[pl.ds(t, n, stride=L), :] for t in range(L)] if L > 1 else [u_ref[...]]
    x32 = jnp.concatenate(xs, axis=1)
    xb = x32.astype(BF16)
    w = w_ref[0]
    s = _dot(xb, w)
    if precise_state:
        xlo = (x32 - xb.astype(F32)).astype(BF16)
        s = s + _dot(xlo, w) + _dot(xb, w_ref[1])
    a1 = a1_ref[...]
    a2 = a2_ref[...]
    h = h_scr[...]
    width = h.shape[1]
    is_re = (lax.broadcasted_iota(jnp.int32, (1, width), 1) % (2 * SSM_STATE)) < SSM_STATE
    for j in range(nj):
        hp_scr[j * nseq:(j + 1) * nseq, :] = h
        swapped = jnp.where(is_re, pltpu.roll(h, width - SSM_STATE, 1), pltpu.roll(h, SSM_STATE, 1))
        h = a1 * h + a2 * swapped + s[j * nseq:(j + 1) * nseq, :]
    h_scr[...] = h
    y = _dot(xb, t_ref[...]) + _dot(hp_scr[...].astype(BF16), v_ref[...])
    if L > 1:
        for t in range(L):
            y_scr[pl.ds(t, n, stride=L), :] = y[:, t * LANES:(t + 1) * LANES]
        ytok = y_scr[...]
    else:
        ytok = y
    ytok = ytok + d_ref[...] * u_ref[...]
    g = _gelu_exact(ytok)
    z = _dot(g.astype(BF16), gw_ref[...]) + gb_ref[...]
    y_ref[...] = (g * jax.nn.sigmoid(z)).astype(y_ref.dtype)

    @pl.when(i == pl.num_programs(1) - 1)
    def _():
        hout_ref[...] = h


def s5_mixer(u_rows, h0_re, h0_im, tables, ssm_d, glu_w, glu_b, *, nseq, chunk_len, chunks_per_tile,
             precise_state):
    tmat, wmat, vmat, a1, a2 = tables
    mrows, width = u_rows.shape
    nblk = width // LANES
    gb = GROUPS_PER_LANE_BLOCK
    L = chunk_len
    rows_tile = chunks_per_tile * nseq * L
    nt = mrows // rows_tile
    g_total = h0_re.shape[1]
    n = h0_re.shape[2]
    h0 = jnp.stack([h0_re.astype(F32), h0_im.astype(F32)], axis=2)
    h0 = h0.reshape(nseq, nblk, gb * 2 * n).transpose(1, 0, 2)
    drow = ssm_d.astype(F32).reshape(nblk, 1, LANES)
    gwb = jnp.einsum('qgce,gh->qgche', glu_w.astype(F32).reshape(nblk, gb, SSM_GROUP, SSM_GROUP),
                     jnp.eye(gb, dtype=F32)).reshape(nblk, LANES, LANES).astype(BF16)
    gbrow = glu_b.astype(F32).reshape(nblk, 1, LANES)
    kl = L * LANES
    sl = STATE_LANES
    blk = lambda q, i: (q, 0, 0)
    w_hi = wmat.astype(BF16)
    if precise_state:
        w_parts = jnp.stack([w_hi, (wmat - w_hi.astype(F32)).astype(BF16)], axis=1)
    else:
        w_parts = w_hi[:, None]
    nparts = w_parts.shape[1]
    y, hout = pl.pallas_call(
        functools.partial(_s5_kernel, chunk_len=L, nseq=nseq, precise_state=precise_state),
        grid=(nblk, nt),
        in_specs=[pl.BlockSpec((rows_tile, LANES), lambda q, i: (i, q)),
                  pl.BlockSpec((None, nseq, sl), blk),
                  pl.BlockSpec((None, kl, kl), blk),
                  pl.BlockSpec((None, nparts, kl, sl), lambda q, i: (q, 0, 0, 0)),
                  pl.BlockSpec((None, sl, kl), blk),
                  pl.BlockSpec((None, 1, sl), blk),
                  pl.BlockSpec((None, 1, sl), blk),
                  pl.BlockSpec((None, 1, LANES), blk),
                  pl.BlockSpec((None, LANES, LANES), blk),
                  pl.BlockSpec((None, 1, LANES), blk)],
        out_specs=[pl.BlockSpec((rows_tile, LANES), lambda q, i: (i, q)),
                   pl.BlockSpec((None, nseq, sl), blk)],
        out_shape=[jax.ShapeDtypeStruct((mrows, width), BF16),
                   jax.ShapeDtypeStruct((nblk, nseq, sl), F32)],
        scratch_shapes=[pltpu.VMEM((nseq, sl), F32),
                        pltpu.VMEM((chunks_per_tile * nseq, sl), F32),
                        pltpu.VMEM((rows_tile, LANES), F32)],
        compiler_params=_cparams("parallel", "arbitrary"),
        name="s5_mixer",
    )(u_rows, h0, tmat.astype(BF16), w_parts, vmat.astype(BF16), a1, a2, drow, gwb, gbrow)
    hout = hout.transpose(1, 0, 2).reshape(nseq, g_total, 2, n)
    return y, hout[:, :, 0, :], hout[:, :, 1, :]


def _lane_shift_down(x, s):
    lanes = x.shape[1]
    lane = lax.broadcasted_iota(jnp.int32, x.shape, 1)
    return jnp.where(lane < lanes - s, pltpu.roll(x, lanes - s, 1), 0.0)


def _lane_shift_up(x, s):
    lane = lax.broadcasted_iota(jnp.int32, x.shape, 1)
    return jnp.where(lane >= s, pltpu.roll(x, s, 1), 0.0)


def _fox_sample_kernel(pt_ref, q_ref, knt_ref, vnt_ref, lfnt_ref, *rest, pages_per_step, n_new):
    pp = pages_per_step
    k_refs = rest[:pp]
    v_refs = rest[pp:2 * pp]
    lf_refs = rest[2 * pp:3 * pp]
    o_ref, m_ref, l_ref, acc_ref, carry_ref, qbd_ref, ccol_ref = rest[3 * pp:]
    g = pl.program_id(1)
    nrow = n_new * N_HEADS
    width = q_ref.shape[1]
    page = lf_refs[0].shape[1]
    headmask = (lax.broadcasted_iota(jnp.int32, (N_HEADS, width), 0)
                == lax.broadcasted_iota(jnp.int32, (N_HEADS, width), 1) // HEAD_DIM)
    lane = lax.broadcasted_iota(jnp.int32, (N_HEADS, page), 1)

    def new_cumsum_t():
        c = lfnt_ref[...]
        s = 1
        while s < n_new:
            c = c + _lane_shift_up(c, s)
            s *= 2
        return c

    def tile_rows(x):
        return jnp.concatenate([x] * n_new, axis=0)

    @pl.when(g == 0)
    def _():
        m_ref[...] = jnp.full_like(m_ref, NEG_INF)
        l_ref[...] = jnp.zeros_like(l_ref)
        acc_ref[...] = jnp.zeros_like(acc_ref)
        carry_ref[...] = jnp.zeros_like(carry_ref)
        q = q_ref[...].astype(F32)
        cn = new_cumsum_t()
        for qi in range(n_new):
            qrow = jnp.broadcast_to(q[qi:qi + 1, :], (N_HEADS, width))
            qbd_ref[qi * N_HEADS:(qi + 1) * N_HEADS, :] = jnp.where(headmask, qrow, 0.0).astype(BF16)
            ccol_ref[qi * N_HEADS:(qi + 1) * N_HEADS, :] = jnp.sum(
                jnp.where(lane == qi, cn, 0.0), axis=1, keepdims=True)

    def update(s_list, vt_list):
        m_old = m_ref[...]
        mx = s_list[0]
        for s in s_list[1:]:
            mx = jnp.maximum(mx, s)
        m_new = jnp.maximum(m_old, jnp.max(mx, axis=1, keepdims=True))
        alpha = jnp.exp(m_old - m_new)
        psum = jnp.zeros(s_list[0].shape, F32)
        pv = jnp.zeros(acc_ref.shape, F32)
        for s, vt in zip(s_list, vt_list):
            p = jnp.exp(s - m_new)
            psum = psum + p
            pv = pv + _dot_nt(p.astype(BF16), vt)
        l_ref[...] = alpha * l_ref[...] + jnp.sum(psum, axis=1, keepdims=True)
        m_ref[...] = m_new
        acc_ref[...] = alpha * acc_ref[...] + pv

    qbd = qbd_ref[...]
    ccol = ccol_ref[...]
    carry = carry_ref[...]
    s_list, vt_list = [], []
    for pi in range(pp):
        kt = k_refs[pi][...]
        kt2 = kt.reshape(kt.shape[0] * kt.shape[1], kt.shape[2]).astype(BF16)
        lf = lf_refs[pi][...]
        rs = _lane_shift_down(lf, 1)
        s = 1
        while s < page:
            rs = rs + _lane_shift_down(rs, s)
            s *= 2
        rs = rs + carry
        carry = carry + jnp.sum(lf, axis=1, keepdims=True)
        s_list.append(_dot(qbd, kt2) + tile_rows(rs) + ccol)
        vt = v_refs[pi][...]
        vt_list.append(vt.reshape(vt.shape[0] * vt.shape[1], vt.shape[2]).astype(BF16))
    carry_ref[...] = carry
    update(s_list, vt_list)

    @pl.when(g == pl.num_programs(1) - 1)
    def _():
        s = _dot(qbd, knt_ref[...]) + ccol - tile_rows(new_cumsum_t())
        kj = lax.broadcasted_iota(jnp.int32, (nrow, page), 1)
        qi = lax.broadcasted_iota(jnp.int32, (nrow, page), 0) // N_HEADS
        s = jnp.where(kj <= qi, s, NEG_INF)
        update([s], [vnt_ref[...]])
        o = acc_ref[...] / l_ref[...]
        for q_i in range(n_new):
            blk = o[q_i * N_HEADS:(q_i + 1) * N_HEADS, :]
            o_ref[q_i:q_i + 1, :] = jnp.sum(jnp.where(headmask, blk, 0.0), axis=0,
                                            keepdims=True).astype(o_ref.dtype)


def fox_sample_attention(qb, kb_new, vb_new, logf_new, cache_k, cache_v, cache_logf, page_table,
                         *, pages_per_step):
    nb, n_new, a = qb.shape
    n_pool, page, nh, hd = cache_k.shape
    n_pages = page_table.shape[1]
    pp = pages_per_step
    assert n_new <= page and nh == N_HEADS
    ckt = cache_k.transpose(0, 2, 3, 1)
    cvt = cache_v.transpose(0, 2, 3, 1)
    clt = cache_logf.transpose(0, 2, 1)
    lane_pad = ((0, 0), (0, 0), (0, page - n_new))
    knt = jnp.pad(kb_new.transpose(0, 2, 1), lane_pad)
    vnt = jnp.pad(vb_new.transpose(0, 2, 1), lane_pad)
    lfnt = jnp.pad(logf_new.transpose(0, 2, 1), lane_pad)
    pt = page_table.reshape(-1).astype(jnp.int32)

    def page_map(pi, ndim):
        return lambda b, g, pt_ref: (pt_ref[b * n_pages + (n_pages - 1 - (g * pp + pi))],) + (0,) * (ndim - 1)

    per_b = lambda b, g, pt_ref: (b, 0, 0)
    nrow = n_new * nh
    grid_spec = pltpu.PrefetchScalarGridSpec(
        num_scalar_prefetch=1,
        grid=(nb, n_pages // pp),
        in_specs=[pl.BlockSpec((None, n_new, a), per_b),
                  pl.BlockSpec((None, a, page), per_b),
                  pl.BlockSpec((None, a, page), per_b),
                  pl.BlockSpec((None, nh, page), per_b)]
                 + [pl.BlockSpec((None, nh, hd, page), page_map(pi, 4)) for pi in range(pp)]
                 + [pl.BlockSpec((None, nh, hd, page), page_map(pi, 4)) for pi in range(pp)]
                 + [pl.BlockSpec((None, nh, page), page_map(pi, 3)) for pi in range(pp)],
        out_specs=pl.BlockSpec((None, n_new, a), per_b),
        scratch_shapes=[pltpu.VMEM((nrow, 1), F32), pltpu.VMEM((nrow, 1), F32),
                        pltpu.VMEM((nrow, a), F32), pltpu.VMEM((nh, 1), F32),
                        pltpu.VMEM((nrow, a), BF16), pltpu.VMEM((nrow, 1), F32)],
    )
    return pl.pallas_call(
        functools.partial(_fox_sample_kernel, pages_per_step=pp, n_new=n_new),
        grid_spec=grid_spec,
        out_shape=jax.ShapeDtypeStruct((nb, n_new, a), BF16),
        compiler_params=_cparams("parallel", "arbitrary"),
        name="fox_sample",
    )(pt, qb, knt, vnt, lfnt, *([ckt] * pp), *([cvt] * pp), *([clt] * pp))


def _layer_norm(x, g, b):
    mu = jnp.mean(x, axis=-1, keepdims=True)
    xc = x - mu
    var = jnp.mean(xc * xc, axis=-1, keepdims=True)
    return xc * lax.rsqrt(var + LN_EPS) * g + b


SUBLANES = 8


def _to_row_tiles(x, ref):
    rows = x.shape[0]
    for c in range(SUBLANES):
        ref[pl.ds(c, rows, stride=SUBLANES), :] = x[:, c * LANES:(c + 1) * LANES]


def _from_row_tiles(ref, first_row, rows):
    return jnp.concatenate(
        [ref[pl.ds(first_row * SUBLANES + c, rows, stride=SUBLANES), :] for c in range(SUBLANES)], axis=1)


def _outproj_router_kernel(*refs, n_experts, aliased, n_tiles):
    if aliased:
        refs = refs[2:]
    ins, (h_ref, ht_ref, idx_ref, w_ref, rank_ref, cnt_out_ref, cnt_scr) = refs[:11], refs[11:]
    cnt_in_ref = ins[10]
    i = pl.program_id(0)

    @pl.when(i == 0)
    def _():
        cnt_scr[...] = cnt_in_ref[...]

    @pl.when(i < n_tiles)
    def _():
        _outproj_router_tile(*ins[:10], h_ref, ht_ref, idx_ref, w_ref, rank_ref, cnt_scr, n_experts=n_experts)

    @pl.when(i >= n_tiles)
    def _():
        h_ref[...] = jnp.zeros_like(h_ref)
        ht_ref[...] = jnp.zeros_like(ht_ref)

    cnt_out_ref[...] = cnt_scr[...]


def _outproj_router_tile(att_ref, ssm_ref, x_ref, woa_ref, wos_ref, g_ref, b_ref, rw_ref, rb_ref,
                         tri_ref, h_ref, ht_ref, idx_ref, w_ref, rank_ref, cnt_scr, *, n_experts):
    tm = x_ref.shape[0]
    att = att_ref[...]
    ssm = ssm_ref[...].reshape(tm, ssm_ref.shape[-1])
    mixed = _dot(att, woa_ref[...]) + _dot(ssm, wos_ref[...])
    h = _layer_norm(DN_ALPHA * x_ref[...] + mixed, g_ref[...], b_ref[...])
    h_ref[...] = h
    _to_row_tiles(h, ht_ref)
    h_hi = h.astype(BF16)
    h_lo = (h - h_hi.astype(F32)).astype(BF16)
    logits = _dot(h_hi, rw_ref[0]) + _dot(h_lo, rw_ref[0]) + _dot(h_hi, rw_ref[1]) + rb_ref[...]
    lane = lax.broadcasted_iota(jnp.int32, logits.shape, 1)
    logits = jnp.where(lane < n_experts, logits, NEG_INF)
    vals, idxs = [], []
    for _ in range(TOP_K):
        mx = jnp.max(logits, axis=1, keepdims=True)
        ix = jnp.min(jnp.where(logits == mx, lane, LANES), axis=1, keepdims=True)
        vals.append(mx)
        idxs.append(ix)
        logits = jnp.where(lane == ix, NEG_INF, logits)
    es = [jnp.exp(v - vals[0]) for v in vals]
    denom = es[0] + es[1] + es[2] + es[3]
    lane8 = lax.broadcasted_iota(jnp.int32, (tm, idx_ref.shape[1]), 1)
    idx_out = jnp.zeros(lane8.shape, jnp.int32)
    w_out = jnp.zeros(lane8.shape, F32)
    onehot = jnp.zeros(logits.shape, F32)
    for k in range(TOP_K):
        onehot = onehot + (lane == idxs[k]).astype(F32)
    before = _dot(tri_ref[...], onehot.astype(BF16)) + cnt_scr[...]
    cnt_scr[...] = cnt_scr[...] + jnp.sum(onehot, axis=0, keepdims=True)
    rank_out = jnp.zeros(lane8.shape, jnp.int32)
    for k in range(TOP_K):
        idx_out = jnp.where(lane8 == k, idxs[k], idx_out)
        w_out = jnp.where(lane8 == k, es[k] / denom, w_out)
        rank_k = jnp.sum(jnp.where(lane == idxs[k], before, 0.0), axis=1, keepdims=True)
        rank_out = jnp.where(lane8 == k, rank_k.astype(jnp.int32), rank_out)
    idx_ref[...] = idx_out
    w_ref[...] = w_out
    rank_ref[...] = rank_out


ROUTE_COLS = 8


def outproj_router(att, ssm, x2, w_out, ln_g, ln_b, router_w, router_b, counts_in, *, tm, m_total,
                   row_block_offset, h_all, ssm_chunked):
    m, d = x2.shape
    a = att.shape[1]
    n_experts = router_w.shape[1]
    nt = m // tm
    woa = w_out[:a].astype(BF16)
    wos = w_out[a:].astype(BF16)
    rw = jnp.pad(router_w.astype(F32), ((0, 0), (0, LANES - n_experts)))
    rw_hi = rw.astype(BF16)
    rw_parts = jnp.stack([rw_hi, (rw - rw_hi.astype(F32)).astype(BF16)])
    rb = jnp.pad(router_b.astype(F32), (0, LANES - n_experts)).reshape(1, LANES)
    aliased = h_all is not None
    n_fill = 0 if aliased else m_total // tm - nt - row_block_offset
    clamp = lambda i: jnp.minimum(i, nt - 1)
    row = lambda i: (clamp(i), 0)
    full = lambda i: (0, 0)
    if ssm_chunked is None:
        ssm_spec = pl.BlockSpec((tm, ssm.shape[-1]), row)
    else:
        nseq, slen, L = ssm_chunked
        tps = slen // tm
        ssm_spec = pl.BlockSpec((tm // L, None, L, ssm.shape[-1]),
                                lambda i: (clamp(i) % tps, clamp(i) // tps, 0, 0))
    in_specs = [pl.BlockSpec((tm, a), row), ssm_spec, pl.BlockSpec((tm, d), row),
                pl.BlockSpec((a, d), full), pl.BlockSpec((w_out.shape[0] - a, d), full),
                pl.BlockSpec((1, d), full), pl.BlockSpec((1, d), full),
                pl.BlockSpec((2, d, LANES), lambda i: (0, 0, 0)), pl.BlockSpec((1, LANES), full),
                pl.BlockSpec((tm, tm), full), pl.BlockSpec((1, LANES), full)]
    tri_strict = _tri_ones(tm, -1)
    args = [att, ssm, x2, woa, wos, ln_g.astype(F32).reshape(1, d), ln_b.astype(F32).reshape(1, d), rw_parts, rb,
            tri_strict, counts_in]
    assert d == SUBLANES * LANES
    aliases = {}
    if aliased:
        in_specs = [pl.BlockSpec(memory_space=pl.ANY)] * 2 + in_specs
        args = list(h_all) + args
        aliases = {0: 0, 1: 1}
    return pl.pallas_call(
        functools.partial(_outproj_router_kernel, n_experts=n_experts, aliased=aliased, n_tiles=nt),
        grid=(nt + n_fill,),
        in_specs=in_specs,
        out_specs=[pl.BlockSpec((tm, d), lambda i: (i + row_block_offset, 0)),
                   pl.BlockSpec((tm * SUBLANES, LANES), lambda i: (i + row_block_offset, 0)),
                   pl.BlockSpec((tm, ROUTE_COLS), row), pl.BlockSpec((tm, ROUTE_COLS), row),
                   pl.BlockSpec((tm, ROUTE_COLS), row), pl.BlockSpec((1, LANES), full)],
        out_shape=[jax.ShapeDtypeStruct((m_total, d), F32),
                   jax.ShapeDtypeStruct((m_total * SUBLANES, LANES), F32),
                   jax.ShapeDtypeStruct((m, ROUTE_COLS), jnp.int32),
                   jax.ShapeDtypeStruct((m, ROUTE_COLS), F32),
                   jax.ShapeDtypeStruct((m, ROUTE_COLS), jnp.int32),
                   jax.ShapeDtypeStruct((1, LANES), F32)],
        input_output_aliases=aliases,
        scratch_shapes=[pltpu.VMEM((1, LANES), F32)],
        compiler_params=_cparams("arbitrary"),
        name="outproj_router",
    )(*args)


DMA_UNROLL = 8


def _for_each_row(n_rows, fn):
    def body(j, carry):
        for u in range(DMA_UNROLL):
            fn(j * DMA_UNROLL + u, u)
        return carry
    lax.fori_loop(0, n_rows // DMA_UNROLL, body, 0)


def _tile_rows(ref, row_tile):
    return ref.at[pl.ds(pl.multiple_of(row_tile * SUBLANES, SUBLANES), SUBLANES), :]


def _scatter_rows_kernel(base_ref, ztile_ref, idx_ref, rank_ref, ht_ref, zeros_ref, xs_ref, sem, zsem, *, tile):
    n_assign = idx_ref.shape[1]
    n_experts = base_ref.shape[0]

    @pl.when(pl.program_id(0) == 0)
    def _():
        def zero_copy(e):
            row0 = pl.multiple_of(ztile_ref[e] * SUBLANES, tile * SUBLANES)
            return pltpu.make_async_copy(zeros_ref, xs_ref.at[pl.ds(row0, tile * SUBLANES), :], zsem)
        for e in range(n_experts):
            zero_copy(e).start()
        for e in range(n_experts):
            zero_copy(e).wait()

    def row_copy(a, dst_row):
        tok = lax.shift_right_logical(a, 2)
        return pltpu.make_async_copy(_tile_rows(ht_ref, tok), _tile_rows(xs_ref, dst_row), sem)

    def start(a, u):
        row_copy(a, base_ref[idx_ref[0, a]] + rank_ref[0, a]).start(priority=u % 2)

    def wait(a, u):
        row_copy(a, 0).wait()

    _for_each_row(n_assign, start)
    _for_each_row(n_assign, wait)


def scatter_rows(h_tiles, idx_flat, rank_flat, base, ztile, *, n_rows_out, tokens_per_tile, expert_tile):
    assert TOP_K == 4
    m = h_tiles.shape[0] // SUBLANES
    nt = m // tokens_per_tile
    na = tokens_per_tile * TOP_K
    grid_spec = pltpu.PrefetchScalarGridSpec(
        num_scalar_prefetch=2,
        grid=(nt,),
        in_specs=[pl.BlockSpec((None, 1, na), lambda i, b, z: (i, 0, 0), memory_space=pltpu.SMEM),
                  pl.BlockSpec((None, 1, na), lambda i, b, z: (i, 0, 0), memory_space=pltpu.SMEM),
                  pl.BlockSpec((tokens_per_tile * SUBLANES, LANES), lambda i, b, z: (i, 0)),
                  pl.BlockSpec((expert_tile * SUBLANES, LANES), lambda i, b, z: (0, 0))],
        out_specs=pl.BlockSpec(memory_space=pl.ANY),
        scratch_shapes=[pltpu.SemaphoreType.DMA, pltpu.SemaphoreType.DMA],
    )
    return pl.pallas_call(
        functools.partial(_scatter_rows_kernel, tile=expert_tile),
        grid_spec=grid_spec,
        out_shape=jax.ShapeDtypeStruct((n_rows_out * SUBLANES, LANES), h_tiles.dtype),
        compiler_params=_cparams("arbitrary"),
        name="moe_scatter",
    )(base, ztile, idx_flat.reshape(nt, 1, na), rank_flat.reshape(nt, 1, na), h_tiles,
      jnp.zeros((expert_tile * SUBLANES, LANES), h_tiles.dtype))


def _expert_kernel(te_ref, tv_ref, x_ref, wup_ref, bup_ref, wdn_ref, bdn_ref, y_ref, wup_bf, wdn_bf):
    i = pl.program_id(0)
    de = wdn_ref.shape[0]
    tm = x_ref.shape[0] // SUBLANES

    @pl.when((i == 0) | (te_ref[i] != te_ref[jnp.maximum(i - 1, 0)]))
    def _():
        wup_bf[...] = wup_ref[...].astype(BF16)
        wdn_bf[...] = wdn_ref[...].astype(BF16)

    @pl.when(tv_ref[i] > 0)
    def _():
        xb = _from_row_tiles(x_ref, 0, tm).astype(BF16)
        hcat = _dot(xb, wup_bf[...]) + bup_ref[...]
        x_glu = jnp.minimum(hcat[:, :de], SWIGLU_LIMIT)
        x_lin = jnp.clip(hcat[:, de:], -SWIGLU_LIMIT, SWIGLU_LIMIT)
        act = x_glu * jax.nn.sigmoid(SWIGLU_ALPHA * x_glu) * (x_lin + 1.0)
        _to_row_tiles(_dot(act.astype(BF16), wdn_bf[...]) + bdn_ref[...], y_ref)

    @pl.when(tv_ref[i] == 0)
    def _():
        y_ref[...] = jnp.zeros_like(y_ref)


def expert_mlp(xs, tile_expert, tile_valid, w_up, b_up, w_down, b_down, *, tm):
    npad = xs.shape[0] // SUBLANES
    n_e, d, d2 = w_up.shape
    de = w_down.shape[1]
    dout = w_down.shape[2]
    assert d == SUBLANES * LANES and dout == SUBLANES * LANES
    nt = npad // tm
    grid_spec = pltpu.PrefetchScalarGridSpec(
        num_scalar_prefetch=2,
        grid=(nt,),
        in_specs=[pl.BlockSpec((tm * SUBLANES, LANES), lambda i, te, tv: (jnp.where(tv[i] > 0, i, te[nt]), 0)),
                  pl.BlockSpec((None, d, d2), lambda i, te, tv: (te[i], 0, 0)),
                  pl.BlockSpec((None, 1, d2), lambda i, te, tv: (te[i], 0, 0)),
                  pl.BlockSpec((None, de, dout), lambda i, te, tv: (te[i], 0, 0)),
                  pl.BlockSpec((None, 1, dout), lambda i, te, tv: (te[i], 0, 0))],
        out_specs=pl.BlockSpec((tm * SUBLANES, LANES), lambda i, te, tv: (i, 0)),
        scratch_shapes=[pltpu.VMEM((d, d2), BF16), pltpu.VMEM((de, dout), BF16)],
    )
    return pl.pallas_call(
        _expert_kernel,
        grid_spec=grid_spec,
        out_shape=jax.ShapeDtypeStruct((npad * SUBLANES, LANES), F32),
        compiler_params=_cparams("arbitrary"),
        name="moe_experts",
    )(tile_expert, tile_valid, xs, w_up.astype(F32),
      b_up.astype(F32).reshape(n_e, 1, d2), w_down.astype(F32), b_down.astype(F32).reshape(n_e, 1, dout))


def _combine_ln_kernel(base_ref, idx_ref, rank_ref, idxn_ref, rankn_ref, h_ref, w_ref, g_ref, b_ref, ys_ref,
                       o_ref, buf0, buf1, sem0, sem1):
    tm = h_ref.shape[0]
    n_assign = idx_ref.shape[1]
    i = pl.program_id(0)
    last = pl.num_programs(0) - 1

    def gather(idx_r, rank_r, buf, sem):
        def row_copy(a, src_row):
            dst = lax.bitwise_and(a, TOP_K - 1) * tm + lax.shift_right_logical(a, 2)
            return pltpu.make_async_copy(_tile_rows(ys_ref, src_row), _tile_rows(buf, dst), sem)

        def start(a, u):
            row_copy(a, base_ref[idx_r[0, a]] + rank_r[0, a]).start(priority=u % 2)

        def wait(a, u):
            row_copy(a, 0).wait()

        return start, wait

    def step(buf_cur, sem_cur, buf_nxt, sem_nxt):
        start_cur, wait_cur = gather(idx_ref, rank_ref, buf_cur, sem_cur)
        start_nxt, _ = gather(idxn_ref, rankn_ref, buf_nxt, sem_nxt)

        @pl.when(i == 0)
        def _():
            _for_each_row(n_assign, start_cur)

        @pl.when(i < last)
        def _():
            _for_each_row(n_assign, start_nxt)

        _for_each_row(n_assign, wait_cur)
        w = w_ref[...]
        moe = w[:, 0:1] * _from_row_tiles(buf_cur, 0, tm)
        for k in range(1, TOP_K):
            moe = moe + w[:, k:k + 1] * _from_row_tiles(buf_cur, k * tm, tm)
        o_ref[...] = _layer_norm(DN_ALPHA * h_ref[...] + moe, g_ref[...], b_ref[...])

    @pl.when(i % 2 == 0)
    def _():
        step(buf0, sem0, buf1, sem1)

    @pl.when(i % 2 == 1)
    def _():
        step(buf1, sem1, buf0, sem0)


def combine_ln(h_all, ys, idx_flat, rank_flat, top_w, base, ln_g, ln_b, *, tm, m, row_block_offset):
    assert TOP_K == 4
    d = h_all.shape[1]
    nt = m // tm
    na = tm * TOP_K
    grid_spec = pltpu.PrefetchScalarGridSpec(
        num_scalar_prefetch=1,
        grid=(nt,),
        in_specs=[pl.BlockSpec((None, 1, na), lambda i, b: (i, 0, 0), memory_space=pltpu.SMEM),
                  pl.BlockSpec((None, 1, na), lambda i, b: (i, 0, 0), memory_space=pltpu.SMEM),
                  pl.BlockSpec((None, 1, na), lambda i, b: (jnp.minimum(i + 1, nt - 1), 0, 0), memory_space=pltpu.SMEM),
                  pl.BlockSpec((None, 1, na), lambda i, b: (jnp.minimum(i + 1, nt - 1), 0, 0), memory_space=pltpu.SMEM),
                  pl.BlockSpec((tm, d), lambda i, b: (i + row_block_offset, 0)),
                  pl.BlockSpec((tm, ROUTE_COLS), lambda i, b: (i, 0)),
                  pl.BlockSpec((1, d), lambda i, b: (0, 0)),
                  pl.BlockSpec((1, d), lambda i, b: (0, 0)),
                  pl.BlockSpec(memory_space=pl.ANY)],
        out_specs=pl.BlockSpec((tm, d), lambda i, b: (i, 0)),
        scratch_shapes=[pltpu.VMEM((TOP_K * tm * SUBLANES, LANES), F32),
                        pltpu.VMEM((TOP_K * tm * SUBLANES, LANES), F32),
                        pltpu.SemaphoreType.DMA, pltpu.SemaphoreType.DMA],
    )
    idx3 = idx_flat.reshape(nt, 1, na)
    rank3 = rank_flat.reshape(nt, 1, na)
    return pl.pallas_call(
        _combine_ln_kernel,
        grid_spec=grid_spec,
        out_shape=jax.ShapeDtypeStruct((m, d), F32),
        compiler_params=_cparams("arbitrary"),
        name="moe_combine_ln",
    )(base, idx3, rank3, idx3, rank3, h_all, top_w,
      ln_g.astype(F32).reshape(1, d), ln_b.astype(F32).reshape(1, d), ys)


def _tile_plan(counts, n_assign, tm):
    n_experts = counts.shape[0]
    n_tiles = n_assign // tm + n_experts
    ntile_e = (counts + tm - 1) // tm
    tile_end = jnp.cumsum(ntile_e)
    tile_start = tile_end - ntile_e
    tile_ids = jnp.arange(n_tiles, dtype=jnp.int32)
    tile_expert = jnp.minimum((tile_ids[:, None] >= tile_end[None, :]).sum(axis=1), n_experts - 1)
    n_used = tile_end[-1]
    tile_valid = (tile_ids < n_used).astype(jnp.int32)
    tile_expert = jnp.concatenate([tile_expert, jnp.maximum(n_used - 1, 0)[None]]).astype(jnp.int32)
    base = (tile_start * tm).astype(jnp.int32)
    ztile = jnp.where(ntile_e > 0, (tile_end - 1) * tm, (n_tiles - 1) * tm).astype(jnp.int32)
    return n_tiles, tile_expert, tile_valid, base, ztile


S5_CHUNK = 16
TOKEN_TILE = 512
ATTN_TILE = 1024
S5_CHUNKS_PER_TILE = 16
PAGES_PER_STEP = 32
EXPERT_TILE = 512
SCATTER_TILE = 512
COMBINE_TILE = 256


def kernel(x_prompt, x_sample, cache_k, cache_v, cache_logf, state_ssm_re, state_ssm_im, page_table, w_in, b_forget, ssm_a_re, ssm_a_im, ssm_log_dt, ssm_b_re, ssm_b_im, ssm_c_re, ssm_c_im, ssm_d, ssm_glu_w, ssm_glu_b, w_out, ln1_g, ln1_b, router_w, router_b, w_up, b_up, w_down, b_down, ln2_g, ln2_b):
    bsz, slen, d = x_prompt.shape
    dec_b, dec_t, _ = x_sample.shape
    m_p = bsz * slen
    m_s = dec_b * dec_t
    m_all = m_p + m_s
    n_groups, n_state = ssm_a_re.shape
    n_experts = router_w.shape[1]
    tm = TOKEN_TILE
    assert m_s == tm and slen % tm == 0 and m_p % tm == 0
    assert (m_all * TOP_K) % EXPERT_TILE == 0 and m_p % COMBINE_TILE == 0 and m_s % COMBINE_TILE == 0

    qb, k_p, v_p, kb, vb, logf_p, u_p = in_projection(x_prompt, w_in, b_forget, tm=tm, chunk_len=S5_CHUNK,
                                                      aug=True)
    att_p = fox_prompt_attention(qb, kb, vb, bsz, slen, tq=ATTN_TILE)
    tables_p = _s5_tables(ssm_a_re, ssm_a_im, ssm_log_dt, ssm_b_re, ssm_b_im, ssm_c_re, ssm_c_im, S5_CHUNK)
    zeros_state = jnp.zeros((bsz, n_groups, n_state), F32)
    ssm_p, hre_p, him_p = s5_mixer(u_p.reshape(m_p, u_p.shape[-1]), zeros_state, zeros_state, tables_p,
                                   ssm_d, ssm_glu_w, ssm_glu_b, nseq=bsz, chunk_len=S5_CHUNK,
                                   chunks_per_tile=S5_CHUNKS_PER_TILE, precise_state=False)

    qs, k_s, v_s, ksb, vsb, logf_s, u_s = in_projection(x_sample.reshape(1, m_s, d), w_in, b_forget,
                                                         tm=tm, chunk_len=None, aug=False)
    a = ATTN_WIDTH
    att_s = fox_sample_attention(qs.reshape(dec_b, dec_t, a), ksb.reshape(dec_b, dec_t, a),
                                 vsb.reshape(dec_b, dec_t, a), logf_s.reshape(dec_b, dec_t, N_HEADS),
                                 cache_k, cache_v, cache_logf, page_table, pages_per_step=PAGES_PER_STEP)
    tables_s = _s5_tables(ssm_a_re, ssm_a_im, ssm_log_dt, ssm_b_re, ssm_b_im, ssm_c_re, ssm_c_im, dec_t)
    ssm_s, hre_s, him_s = s5_mixer(u_s, state_ssm_re, state_ssm_im, tables_s, ssm_d, ssm_glu_w, ssm_glu_b,
                                   nseq=dec_b, chunk_len=dec_t, chunks_per_tile=1, precise_state=True)

    ssm_p4 = ssm_p.reshape(slen // S5_CHUNK, bsz, S5_CHUNK, ssm_p.shape[-1])
    zero_counts = jnp.zeros((1, LANES), F32)
    h_all, h_tiles, idx_p, w_p, rank_p, cnt_p = outproj_router(
        att_p, ssm_p4, x_prompt.reshape(m_p, d), w_out, ln1_g, ln1_b, router_w, router_b, zero_counts,
        tm=tm, m_total=m_all, row_block_offset=0, h_all=None, ssm_chunked=(bsz, slen, S5_CHUNK))
    h_all, h_tiles, idx_s, w_s, rank_s, cnt_all = outproj_router(
        att_s.reshape(m_s, a), ssm_s, x_sample.reshape(m_s, d), w_out, ln1_g, ln1_b, router_w, router_b, cnt_p,
        tm=tm, m_total=m_all, row_block_offset=m_p // tm, h_all=(h_all, h_tiles), ssm_chunked=None)

    counts = cnt_all[0, :n_experts].astype(jnp.int32)
    n_tiles, tile_expert, tile_valid, base, ztile = _tile_plan(counts, m_all * TOP_K, EXPERT_TILE)
    flat = lambda t: t[:, :TOP_K].reshape(-1)
    idx_pf, idx_sf, rank_pf, rank_sf = flat(idx_p), flat(idx_s), flat(rank_p), flat(rank_s)
    xs = scatter_rows(h_tiles, jnp.concatenate([idx_pf, idx_sf]), jnp.concatenate([rank_pf, rank_sf]), base, ztile,
                      n_rows_out=n_tiles * EXPERT_TILE, tokens_per_tile=SCATTER_TILE, expert_tile=EXPERT_TILE)
    ys = expert_mlp(xs, tile_expert, tile_valid, w_up, b_up, w_down, b_down, tm=EXPERT_TILE)
    y_p = combine_ln(h_all, ys, idx_pf, rank_pf, w_p, base, ln2_g, ln2_b, tm=COMBINE_TILE, m=m_p,
                     row_block_offset=0)
    y_s = combine_ln(h_all, ys, idx_sf, rank_sf, w_s, base, ln2_g, ln2_b, tm=COMBINE_TILE, m=m_s,
                     row_block_offset=m_p // COMBINE_TILE)

    hd = (N_HEADS, HEAD_DIM)
    return (y_p.reshape(bsz, slen, d), y_s.reshape(dec_b, dec_t, d),
            k_p.reshape(bsz, *hd, slen).transpose(0, 3, 1, 2), v_p.reshape(bsz, *hd, slen).transpose(0, 3, 1, 2),
            logf_p.reshape(bsz, slen, N_HEADS).astype(cache_logf.dtype),
            hre_p.astype(state_ssm_re.dtype), him_p.astype(state_ssm_im.dtype),
            k_s.reshape(dec_b, dec_t, *hd), v_s.reshape(dec_b, dec_t, *hd),
            logf_s.reshape(dec_b, dec_t, N_HEADS).astype(cache_logf.dtype),
            hre_s.astype(state_ssm_re.dtype), him_s.astype(state_ssm_im.dtype))
```
